```python
import jax, jax.numpy as jnp
from jax import lax
import numpy as np

D_MODEL = 4096
BATCH = 4
SEQ = 2048
DEPTH = 4

N_MIXERS = 3
N_CONV_LAYERS = (DEPTH + 2) // 3
N_GMLP_LAYERS = (DEPTH + 1) // 3
N_SB_LAYERS = DEPTH // 3
N_DENSE_LAYERS = (DEPTH + 1) // 2
N_MOE_LAYERS = DEPTH // 2

CONV_WIDTH = 3
CHUNK = 128
GMLP_INNER = D_MODEL
GMLP_GROUP_DIM = 128
GMLP_GROUPS = GMLP_INNER // GMLP_GROUP_DIM
SB_HEAD_DIM = 128
SB_HEADS = D_MODEL // SB_HEAD_DIM
SB_BLOCK = 128
D_FF = 2 * D_MODEL
N_EXPERTS = 8
TOP_K = 2
D_EXPERT = D_MODEL // 2
NORM_EPS = 1e-5

kernel_name = "hybrid_conv_gmlp_stickbreak_moe_trunk"


def rms_norm(x, g):
    xf = x.astype(jnp.float32)
    y = xf * lax.rsqrt(jnp.mean(xf * xf, axis=-1, keepdims=True) + NORM_EPS)
    return (y * g.astype(jnp.float32)).astype(x.dtype)


def layer_norm(x, g, b):
    xf = x.astype(jnp.float32)
    mu = jnp.mean(xf, axis=-1, keepdims=True)
    var = jnp.mean(jnp.square(xf - mu), axis=-1, keepdims=True)
    y = (xf - mu) * lax.rsqrt(var + NORM_EPS)
    return (y * g.astype(jnp.float32) + b.astype(jnp.float32)).astype(x.dtype)


def short_conv_mixer(h, w_in, conv_w, w_out):
    bcx = h @ w_in
    b_gate, c_gate, xin = jnp.split(bcx, 3, axis=-1)
    z = c_gate * xin
    d = z.shape[-1]
    zc = lax.conv_general_dilated(
        z, conv_w[:, None, :].astype(z.dtype),
        window_strides=(1,), padding=[(CONV_WIDTH - 1, 0)],
        dimension_numbers=("NWC", "WIO", "NWC"), feature_group_count=d)
    return (b_gate * zc) @ w_out


def chunked_gmlp_mixer(h, w_in, v_norm_g, v_norm_b, w_s, b_s, w_out):
    bsz, s, _ = h.shape
    uv = jax.nn.gelu(h @ w_in, approximate=False)
    u, v = jnp.split(uv, 2, axis=-1)
    v = layer_norm(v, v_norm_g, v_norm_b)
    v = v.reshape(bsz, s // CHUNK, CHUNK, GMLP_GROUPS, GMLP_GROUP_DIM)
    causal = jnp.tril(jnp.ones((CHUNK, CHUNK), dtype=bool))
    ws = jnp.where(causal[None], w_s, 0).astype(v.dtype)
    sv = jnp.einsum("gts,bnsgc->bntgc", ws, v) + b_s.T[:, :, None].astype(v.dtype)
    sv = sv.reshape(bsz, s, GMLP_INNER)
    return (u * sv) @ w_out


def stick_breaking_attention(h, w_qkv, w_out):
    bsz, s, _ = h.shape
    qkv = h @ w_qkv
    q, k, v = jnp.split(qkv, 3, axis=-1)
    q = q.reshape(bsz, s, SB_HEADS, SB_HEAD_DIM)
    k = k.reshape(bsz, s, SB_HEADS, SB_HEAD_DIM)
    v = v.reshape(bsz, s, SB_HEADS, SB_HEAD_DIM)
    scale = SB_HEAD_DIM ** -0.5
    outs = []
    for qb in range(s // SB_BLOCK):
        t0 = qb * SB_BLOCK
        kv_len = t0 + SB_BLOCK
        q_blk = q[:, t0:kv_len]
        k_blk = k[:, :kv_len]
        v_blk = v[:, :kv_len]
        z = jnp.einsum("bqhd,bkhd->bhqk", q_blk, k_blk).astype(jnp.float32) * scale
        t_pos = t0 + jnp.arange(SB_BLOCK, dtype=jnp.int32)[:, None]
        s_pos = jnp.arange(kv_len, dtype=jnp.int32)[None, :]
        strict = s_pos < t_pos
        log_beta = jax.nn.log_sigmoid(z)
        log_1m_beta = jnp.where(strict, jax.nn.log_sigmoid(-z), 0.0)
        shifted = jnp.concatenate(
            [log_1m_beta[..., 1:], jnp.zeros_like(log_1m_beta[..., :1])], axis=-1)
        after = lax.cumsum(shifted, axis=shifted.ndim - 1, reverse=True)
        att = jnp.where(strict, jnp.exp(log_beta + after), 0.0)
        outs.append(jnp.einsum("bhqk,bkhd->bqhd", att.astype(v.dtype), v_blk))
    o = jnp.concatenate(outs, axis=1).reshape(bsz, s, SB_HEADS * SB_HEAD_DIM)
    return o @ w_out


def swiglu(h, w_gu, w_down):
    g, u = jnp.split(h @ w_gu, 2, axis=-1)
    return (jax.nn.silu(g) * u) @ w_down


def moe_swiglu(h, w_router, w_gu, w_down):
    bsz, s, d = h.shape
    t = h.reshape(bsz * s, d)
    logits = (t @ w_router).astype(jnp.float32)
    top_val, top_idx = lax.top_k(logits, TOP_K)
    gates = jax.nn.softmax(top_val, axis=-1)
    combine = jnp.sum(gates[..., None] * jax.nn.one_hot(top_idx, N_EXPERTS, dtype=jnp.float32), axis=1)
    out = jnp.zeros_like(t)
    for e in range(N_EXPERTS):
        g, u = jnp.split(t @ w_gu[e], 2, axis=-1)
        y_e = (jax.nn.silu(g) * u) @ w_down[e]
        out = out + combine[:, e:e + 1].astype(t.dtype) * y_e
    return out.reshape(bsz, s, d)


def setup_inputs(seed: int = 0) -> dict:
    key = jax.random.key(seed)
    ks = jax.random.split(key, 24)
    D = D_MODEL
    res = (2 * DEPTH) ** -0.5

    def nrm(k, shape, scale):
        return jax.random.normal(k, shape, jnp.float32) * scale

    return {
        "x": nrm(ks[0], (BATCH, SEQ, D), 1.0),
        "mix_norm": 1.0 + nrm(ks[1], (DEPTH, D), 0.1),
        "ffn_norm": 1.0 + nrm(ks[2], (DEPTH, D), 0.1),
        "conv_w_in": nrm(ks[3], (N_CONV_LAYERS, D, 3 * D), D ** -0.5),
        "conv_w": nrm(ks[4], (N_CONV_LAYERS, CONV_WIDTH, D), CONV_WIDTH ** -0.5),
        "conv_w_out": nrm(ks[5], (N_CONV_LAYERS, D, D), D ** -0.5 * res),
        "gmlp_w_in": nrm(ks[6], (N_GMLP_LAYERS, D, 2 * GMLP_INNER), D ** -0.5),
        "gmlp_v_norm_g": 1.0 + nrm(ks[7], (N_GMLP_LAYERS, GMLP_INNER), 0.1),
        "gmlp_v_norm_b": nrm(ks[8], (N_GMLP_LAYERS, GMLP_INNER), 0.02),
        "gmlp_w_s": nrm(ks[9], (N_GMLP_LAYERS, GMLP_GROUPS, CHUNK, CHUNK), CHUNK ** -0.5),
        "gmlp_b_s": 1.0 + nrm(ks[10], (N_GMLP_LAYERS, GMLP_GROUPS, CHUNK), 0.1),
        "gmlp_w_out": nrm(ks[11], (N_GMLP_LAYERS, GMLP_INNER, D), GMLP_INNER ** -0.5 * res),
        "sb_w_qkv": nrm(ks[12], (N_SB_LAYERS, D, 3 * SB_HEADS * SB_HEAD_DIM), D ** -0.5),
        "sb_w_out": nrm(ks[13], (N_SB_LAYERS, SB_HEADS * SB_HEAD_DIM, D), D ** -0.5 * res),
        "dense_w_gu": nrm(ks[14], (N_DENSE_LAYERS, D, 2 * D_FF), D ** -0.5),
        "dense_w_down": nrm(ks[15], (N_DENSE_LAYERS, D_FF, D), D_FF ** -0.5 * res),
        "moe_w_router": nrm(ks[16], (N_MOE_LAYERS, D, N_EXPERTS), D ** -0.5),
        "moe_w_gu": nrm(ks[17], (N_MOE_LAYERS, N_EXPERTS, D, 2 * D_EXPERT), D ** -0.5),
        "moe_w_down": nrm(ks[18], (N_MOE_LAYERS, N_EXPERTS, D_EXPERT, D), D_EXPERT ** -0.5 * res),
        "final_norm": 1.0 + nrm(ks[19], (D,), 0.1),
    }


def reference(x, mix_norm, ffn_norm, conv_w_in, conv_w, conv_w_out,
              gmlp_w_in, gmlp_v_norm_g, gmlp_v_norm_b, gmlp_w_s, gmlp_b_s, gmlp_w_out,
              sb_w_qkv, sb_w_out, dense_w_gu, dense_w_down,
              moe_w_router, moe_w_gu, moe_w_down, final_norm):
    for i in range(DEPTH):
        h = rms_norm(x, mix_norm[i])
        kind = i % N_MIXERS
        j = i // N_MIXERS
        if kind == 0:
            y = short_conv_mixer(h, conv_w_in[j], conv_w[j], conv_w_out[j])
        elif kind == 1:
            y = chunked_gmlp_mixer(h, gmlp_w_in[j], gmlp_v_norm_g[j], gmlp_v_norm_b[j],
                                   gmlp_w_s[j], gmlp_b_s[j], gmlp_w_out[j])
        else:
            y = stick_breaking_attention(h, sb_w_qkv[j], sb_w_out[j])
        x = x + y
        h = rms_norm(x, ffn_norm[i])
        f = i // 2
        if i % 2 == 0:
            y = swiglu(h, dense_w_gu[f], dense_w_down[f])
        else:
            y = moe_swiglu(h, moe_w_router[f], moe_w_gu[f], moe_w_down[f])
        x = x + y
    return rms_norm(x, final_norm)
```

```python
import functools

import jax
import jax.numpy as jnp
from jax import lax
from jax.experimental import pallas as pl
from jax.experimental.pallas import tpu as pltpu

F32 = jnp.float32
BF16 = jnp.bfloat16

NORM_EPS = 1e-5
CONV_WIDTH = 3
CHUNK = 128
GROUP_DIM = 128
HEAD_DIM = 128
N_EXPERTS = 8
LANES = 128
VMEM_LIMIT = 56 * 1024 * 1024
SQRT_HALF = 0.7071067811865476

TM = 1024
TN = 512
TN_PART = 256
TQ = 256


def _params(n_axes):
    return pltpu.CompilerParams(
        dimension_semantics=("arbitrary",) * n_axes, vmem_limit_bytes=VMEM_LIMIT)


def _rmsnorm_kernel(x_ref, g_ref, o_ref):
    x = x_ref[...]
    ms = jnp.mean(x * x, axis=-1, keepdims=True)
    o_ref[...] = (x * lax.rsqrt(ms + NORM_EPS) * g_ref[...]).astype(o_ref.dtype)


def rmsnorm(x, g, out_dtype, tr=256):
    m, d = x.shape
    return pl.pallas_call(
        _rmsnorm_kernel,
        grid=(m // tr,),
        in_specs=[pl.BlockSpec((tr, d), lambda i: (i, 0)),
                  pl.BlockSpec((1, d), lambda i: (0, 0))],
        out_specs=pl.BlockSpec((tr, d), lambda i: (i, 0)),
        out_shape=jax.ShapeDtypeStruct((m, d), out_dtype),
        compiler_params=_params(1),
        name="rmsnorm",
    )(x, g.reshape(1, d))


def _rmsnorm_router_kernel(x_ref, g_ref, wr_ref, h_ref, comb_ref):
    x = x_ref[...]
    ms = jnp.mean(x * x, axis=-1, keepdims=True)
    h = x * lax.rsqrt(ms + NORM_EPS) * g_ref[...]
    h_ref[...] = h.astype(h_ref.dtype)
    logits = jnp.dot(h, wr_ref[...], preferred_element_type=F32,
                     precision=lax.Precision.HIGHEST)
    lane = lax.broadcasted_iota(jnp.int32, logits.shape, 1)
    neg = jnp.float32(-jnp.inf)
    logits = jnp.where(lane < N_EXPERTS, logits, neg)
    v1 = jnp.max(logits, axis=-1, keepdims=True)
    i1 = jnp.min(jnp.where(logits == v1, lane, LANES), axis=-1, keepdims=True)
    rest = jnp.where(lane == i1, neg, logits)
    v2 = jnp.max(rest, axis=-1, keepdims=True)
    i2 = jnp.min(jnp.where(rest == v2, lane, LANES), axis=-1, keepdims=True)
    e2 = jnp.exp(v2 - v1)
    denom = 1.0 + e2
    g1 = 1.0 / denom
    g2 = e2 / denom
    comb_ref[...] = jnp.where(lane == i1, g1, 0.0) + jnp.where(lane == i2, g2, 0.0)


def rmsnorm_router(x, g, w_router, tr=256):
    m, d = x.shape
    e = w_router.shape[1]
    wr = jnp.zeros((d, LANES), F32).at[:, :e].set(w_router)
    return pl.pallas_call(
        _rmsnorm_router_kernel,
        grid=(m // tr,),
        in_specs=[pl.BlockSpec((tr, d), lambda i: (i, 0)),
                  pl.BlockSpec((1, d), lambda i: (0, 0)),
                  pl.BlockSpec((d, LANES), lambda i: (0, 0))],
        out_specs=[pl.BlockSpec((tr, d), lambda i: (i, 0)),
                   pl.BlockSpec((tr, LANES), lambda i: (i, 0))],
        out_shape=[jax.ShapeDtypeStruct((m, d), BF16),
                   jax.ShapeDtypeStruct((m, LANES), F32)],
        compiler_params=_params(1),
        name="rmsnorm_router",
    )(x, g.reshape(1, d), wr)


def _mm_kernel(*refs, n_parts, n_extra, n_scratch, epilogue):
    x_ref = refs[0]
    w_refs = refs[1:1 + n_parts]
    extra_refs = refs[1 + n_parts:1 + n_parts + n_extra]
    o_ref = refs[1 + n_parts + n_extra]
    wb_ref = refs[2 + n_parts + n_extra]
    scratch = refs[3 + n_parts + n_extra:]
    assert len(scratch) == n_scratch

    @pl.when(pl.program_id(1) == 0)
    def _cast_weights():
        for p in range(n_parts):
            wb_ref[p] = w_refs[p][...].astype(BF16)

    x = x_ref[...]
    accs = [jnp.dot(x, wb_ref[p], preferred_element_type=F32) for p in range(n_parts)]
    epilogue(accs, extra_refs, o_ref, scratch)


def matmul(x, w, *, w_lead, kb_x, kb_w, k, part_blocks, tn, tm, n_out, out_dtype,
           epilogue, extras=(), extra_specs=(), scratch_shapes=(), name):
    m = x.shape[0]
    n_parts = len(part_blocks)
    grid = (n_out // tn, m // tm)
    in_specs = [pl.BlockSpec((tm, k), lambda n, i: (i, kb_x))]
    for pb in part_blocks:
        in_specs.append(pl.BlockSpec((None, k, tn),
                                     lambda n, i, pb=pb: (w_lead, kb_w, pb + n)))
    in_specs.extend(extra_specs)
    kern = functools.partial(_mm_kernel, n_parts=n_parts, n_extra=len(extras),
                             n_scratch=len(scratch_shapes), epilogue=epilogue)
    return pl.pallas_call(
        kern,
        grid=grid,
        in_specs=in_specs,
        out_specs=pl.BlockSpec((tm, tn), lambda n, i: (i, n)),
        out_shape=jax.ShapeDtypeStruct((m, n_out), out_dtype),
        scratch_shapes=[pltpu.VMEM((n_parts, k, tn), BF16), *scratch_shapes],
        compiler_params=_params(2),
        name=name,
    )(x, *([w] * n_parts), *extras)


def _ep_store(accs, extra_refs, o_ref, scratch):
    o_ref[...] = accs[0].astype(o_ref.dtype)


def _ep_residual(accs, extra_refs, o_ref, scratch):
    o_ref[...] = extra_refs[0][...] + accs[0]


def _ep_swiglu(accs, extra_refs, o_ref, scratch):
    g, u = accs
    o_ref[...] = (g * jax.nn.sigmoid(g) * u).astype(o_ref.dtype)


def _ep_gelu(accs, extra_refs, o_ref, scratch):
    a = accs[0]
    o_ref[...] = (0.5 * a * (1.0 + lax.erf(a * SQRT_HALF))).astype(o_ref.dtype)


def _ep_weighted_residual(accs, extra_refs, o_ref, scratch, *, expert):
    res_ref, comb_ref = extra_refs
    o_ref[...] = res_ref[...] + comb_ref[:, expert:expert + 1] * accs[0]


def _ep_conv(accs, extra_refs, o_ref, scratch, *, tiles_per_seq):
    b, c, xin = accs
    cw_ref, = extra_refs
    tail_ref, = scratch
    z = c * xin
    tm = z.shape[0]
    i = pl.program_id(1)

    @pl.when(i % tiles_per_seq == 0)
    def _zero_tail():
        tail_ref[...] = jnp.zeros_like(tail_ref)

    row = lax.broadcasted_iota(jnp.int32, z.shape, 0)
    t1 = tail_ref[7:8, :]
    t2 = tail_ref[6:7, :]
    z1 = jnp.where(row == 0, t1, pltpu.roll(z, 1, axis=0))
    z2 = jnp.where(row == 0, t2, jnp.where(row == 1, t1, pltpu.roll(z, 2, axis=0)))
    zc = cw_ref[0:1, :] * z2 + cw_ref[1:2, :] * z1 + cw_ref[2:3, :] * z
    o_ref[...] = (b * zc).astype(o_ref.dtype)
    tail_ref[...] = z[tm - 8:, :]


def _res_spec(tm, tn):
    return pl.BlockSpec((tm, tn), lambda n, i: (i, n))


def linear_residual(a, w, res, *, w_lead, kb_x=0, kb_w=0, k, name):
    n_out = w.shape[-1]
    tm, tn = TM, TN
    return matmul(a, w, w_lead=w_lead, kb_x=kb_x, kb_w=kb_w, k=k, part_blocks=(0,), tn=tn,
                  tm=tm, n_out=n_out, out_dtype=F32, epilogue=_ep_residual,
                  extras=(res,), extra_specs=(_res_spec(tm, tn),), name=name)


def short_conv_mixer(h, x, w_in, conv_w, w_out, j, seq):
    d = h.shape[1]
    tm, tn = min(TM, seq), TN_PART
    nb = d // tn
    cw = jnp.zeros((8, d), F32).at[:CONV_WIDTH].set(conv_w[j])
    gated = matmul(
        h, w_in, w_lead=j, kb_x=0, kb_w=0, k=d, part_blocks=(0, nb, 2 * nb), tn=tn, tm=tm,
        n_out=d, out_dtype=BF16,
        epilogue=functools.partial(_ep_conv, tiles_per_seq=seq // tm),
        extras=(cw,), extra_specs=(pl.BlockSpec((8, tn), lambda n, i: (0, n)),),
        scratch_shapes=(pltpu.VMEM((8, tn), F32),), name="conv_in")
    return linear_residual(gated, w_out, x, w_lead=j, k=d, name="conv_out")


def _gmlp_spatial_kernel(u_ref, v_ref, g_ref, b_ref, ws_ref, bst_ref, o_ref):
    v = v_ref[...]
    mu = jnp.mean(v, axis=-1, keepdims=True)
    vc = v - mu
    var = jnp.mean(vc * vc, axis=-1, keepdims=True)
    vn = (vc * lax.rsqrt(var + NORM_EPS) * g_ref[...] + b_ref[...]).astype(BF16)
    t_idx = lax.broadcasted_iota(jnp.int32, (CHUNK, CHUNK), 0)
    s_idx = lax.broadcasted_iota(jnp.int32, (CHUNK, CHUNK), 1)
    causal = s_idx <= t_idx
    n_groups = ws_ref.shape[0]
    for g in range(n_groups):
        sl = slice(g * GROUP_DIM, (g + 1) * GROUP_DIM)
        wm = jnp.where(causal, ws_ref[g], 0.0).astype(BF16)
        sv = jnp.dot(wm, vn[:, sl], preferred_element_type=F32) + bst_ref[:, g:g + 1]
        o_ref[:, sl] = (u_ref[:, sl] * sv).astype(o_ref.dtype)


def gmlp_spatial(uv, v_norm_g, v_norm_b, w_s, b_s):
    m = uv.shape[0]
    inner = uv.shape[1] // 2
    n_groups = inner // GROUP_DIM
    return pl.pallas_call(
        _gmlp_spatial_kernel,
        grid=(m // CHUNK,),
        in_specs=[pl.BlockSpec((CHUNK, inner), lambda i: (i, 0)),
                  pl.BlockSpec((CHUNK, inner), lambda i: (i, 1)),
                  pl.BlockSpec((1, inner), lambda i: (0, 0)),
                  pl.BlockSpec((1, inner), lambda i: (0, 0)),
                  pl.BlockSpec((n_groups, CHUNK, CHUNK), lambda i: (0, 0, 0)),
                  pl.BlockSpec((CHUNK, n_groups), lambda i: (0, 0))],
        out_specs=pl.BlockSpec((CHUNK, inner), lambda i: (i, 0)),
        out_shape=jax.ShapeDtypeStruct((m, inner), BF16),
        compiler_params=_params(1),
        name="gmlp_spatial",
    )(uv, uv, v_norm_g.reshape(1, inner), v_norm_b.reshape(1, inner), w_s, b_s.T)


def chunked_gmlp_mixer(h, x, w_in, v_norm_g, v_norm_b, w_s, b_s, w_out, j):
    d = h.shape[1]
    inner2 = w_in.shape[-1]
    uv = matmul(h, w_in, w_lead=j, kb_x=0, kb_w=0, k=d, part_blocks=(0,), tn=TN, tm=TM,
                n_out=inner2, out_dtype=F32, epilogue=_ep_gelu, name="gmlp_in")
    gated = gmlp_spatial(uv, v_norm_g[j], v_norm_b[j], w_s[j], b_s[j])
    return linear_residual(gated, w_out, x, w_lead=j, k=inner2 // 2, name="gmlp_out")


def _sb_attn_kernel(q_ref, k_ref, v_ref, o_ref, *, tq, scale):
    seq = q_ref.shape[0]
    n_blk = seq // tq
    r = lax.broadcasted_iota(jnp.int32, (tq, tq), 0)
    c = lax.broadcasted_iota(jnp.int32, (tq, tq), 1)
    strict = c < r
    suffix = (r >= c).astype(BF16)

    def scores(qb, kb):
        z = lax.dot_general(qb, kb, (((1,), (1,)), ((), ())), preferred_element_type=F32)
        z = z * scale
        sp = jnp.maximum(z, 0.0) + jnp.log(1.0 + jnp.exp(-jnp.abs(z)))
        return z, sp

    def suffix_sum(sp):
        hi = sp.astype(BF16)
        lo = (sp - hi.astype(F32)).astype(BF16)
        return (jnp.dot(hi, suffix, preferred_element_type=F32)
                + jnp.dot(lo, suffix, preferred_element_type=F32))

    def q_block(qi, _):
        q0 = pl.multiple_of(qi * tq, tq)
        qb = q_ref[pl.ds(q0, tq), :]
        z, sp = scores(qb, k_ref[pl.ds(q0, tq), :])
        sp = jnp.where(strict, sp, 0.0)
        att = jnp.where(strict, jnp.exp(z - suffix_sum(sp)), 0.0)
        acc = jnp.dot(att.astype(BF16), v_ref[pl.ds(q0, tq), :], preferred_element_type=F32)
        carry = jnp.sum(sp, axis=-1, keepdims=True)

        def kv_block(jj, state):
            acc, carry = state
            k0 = pl.multiple_of((qi - 1 - jj) * tq, tq)
            z, sp = scores(qb, k_ref[pl.ds(k0, tq), :])
            att = jnp.exp(z - suffix_sum(sp) - carry)
            acc = acc + jnp.dot(att.astype(BF16), v_ref[pl.ds(k0, tq), :],
                                preferred_element_type=F32)
            carry = carry + jnp.sum(sp, axis=-1, keepdims=True)
            return acc, carry

        acc, _ = lax.fori_loop(0, qi, kv_block, (acc, carry))
        o_ref[pl.ds(q0, tq), :] = acc.astype(o_ref.dtype)
        return 0

    lax.fori_loop(0, n_blk, q_block, 0)


def sb_attention(qkv, batch, seq):
    m, three_d = qkv.shape
    n_heads = three_d // (3 * HEAD_DIM)
    kern = functools.partial(_sb_attn_kernel, tq=TQ, scale=HEAD_DIM ** -0.5)
    return pl.pallas_call(
        kern,
        grid=(batch, n_heads),
        in_specs=[pl.BlockSpec((seq, HEAD_DIM), lambda b, h: (b, h)),
                  pl.BlockSpec((seq, HEAD_DIM), lambda b, h: (b, n_heads + h)),
                  pl.BlockSpec((seq, HEAD_DIM), lambda b, h: (b, 2 * n_heads + h))],
        out_specs=pl.BlockSpec((seq, HEAD_DIM), lambda b, h: (b, h)),
        out_shape=jax.ShapeDtypeStruct((m, n_heads * HEAD_DIM), BF16),
        compiler_params=_params(2),
        name="sb_attention",
    )(qkv, qkv, qkv)


def stick_breaking_mixer(h, x, w_qkv, w_out, j, batch, seq):
    d = h.shape[1]
    qkv = matmul(h, w_qkv, w_lead=j, kb_x=0, kb_w=0, k=d, part_blocks=(0,), tn=TN, tm=TM,
                 n_out=w_qkv.shape[-1], out_dtype=BF16, epilogue=_ep_store, name="sb_qkv")
    o = sb_attention(qkv, batch, seq)
    return linear_residual(o, w_out, x, w_lead=j, k=w_out.shape[1], name="sb_out")


def swiglu_hidden(h, w_gu, w_lead, name):
    d = h.shape[1]
    f = w_gu.shape[-1] // 2
    tm, tn = TM, TN_PART
    return matmul(h, w_gu, w_lead=w_lead, kb_x=0, kb_w=0, k=d, part_blocks=(0, f // tn),
                  tn=tn, tm=tm, n_out=f, out_dtype=BF16, epilogue=_ep_swiglu, name=name)


def dense_swiglu(h, x, w_gu, w_down, f_idx):
    k_split = w_gu.shape[1]
    a = swiglu_hidden(h, w_gu, f_idx, "dense_gu")
    for kb in range(w_down.shape[1] // k_split):
        x = linear_residual(a, w_down, x, w_lead=f_idx, kb_x=kb, kb_w=kb, k=k_split,
                            name="dense_down")
    return x


def moe_swiglu(h, comb, x, w_gu, w_down, f_idx):
    tm, tn = TM, TN
    n_exp, d, _ = w_gu.shape[1:]
    d_exp = w_down.shape[2]
    w_gu = w_gu.reshape(-1, d, w_gu.shape[-1])
    w_down = w_down.reshape(-1, d_exp, d)
    for e in range(n_exp):
        lead = f_idx * n_exp + e
        a = swiglu_hidden(h, w_gu, lead, "moe_gu")
        x = matmul(a, w_down, w_lead=lead, kb_x=0, kb_w=0, k=d_exp, part_blocks=(0,), tn=tn,
                   tm=tm, n_out=d, out_dtype=F32,
                   epilogue=functools.partial(_ep_weighted_residual, expert=e),
                   extras=(x, comb),
                   extra_specs=(_res_spec(tm, tn),
                                pl.BlockSpec((tm, LANES), lambda n, i: (i, 0))),
                   name="moe_down")
    return x


def kernel(x, mix_norm, ffn_norm, conv_w_in, conv_w, conv_w_out, gmlp_w_in, gmlp_v_norm_g,
           gmlp_v_norm_b, gmlp_w_s, gmlp_b_s, gmlp_w_out, sb_w_qkv, sb_w_out, dense_w_gu,
           dense_w_down, moe_w_router, moe_w_gu, moe_w_down, final_norm):
    batch, seq, d = x.shape
    depth = mix_norm.shape[0]
    x = x.reshape(batch * seq, d)
    for i in range(depth):
        h = rmsnorm(x, mix_norm[i], BF16)
        kind, j = i % 3, i // 3
        if kind == 0:
            x = short_conv_mixer(h, x, conv_w_in, conv_w, conv_w_out, j, seq)
        elif kind == 1:
            x = chunked_gmlp_mixer(h, x, gmlp_w_in, gmlp_v_norm_g, gmlp_v_norm_b, gmlp_w_s,
                                   gmlp_b_s, gmlp_w_out, j)
        else:
            x = stick_breaking_mixer(h, x, sb_w_qkv, sb_w_out, j, batch, seq)
        f = i // 2
        if i % 2 == 0:
            h = rmsnorm(x, ffn_norm[i], BF16)
            x = dense_swiglu(h, x, dense_w_gu, dense_w_down, f)
        else:
            h, comb = rmsnorm_router(x, ffn_norm[i], moe_w_router[f])
            x = moe_swiglu(h, comb, x, moe_w_gu, moe_w_down, f)
    return rmsnorm(x, final_norm, F32).reshape(batch, seq, d)
```

```python
import functools

import jax
import jax.numpy as jnp
from jax import lax
from jax.experimental import pallas as pl
from jax.experimental.pallas import tpu as pltpu

F32 = jnp.float32
BF16 = jnp.bfloat16

NORM_EPS = 1e-5
CONV_WIDTH = 3
CHUNK = 128
GROUP_DIM = 128
HEAD_DIM = 128
N_EXPERTS = 8
LANES = 128
VMEM_LIMIT = 56 * 1024 * 1024
SQRT_HALF = 0.7071067811865476

TM = 1024
TN = 512
TN_PART = 256
TQ = 256
SB_HEADS_PER_STEP = 4
TMG = 512


def _params(n_axes):
    return pltpu.CompilerParams(
        dimension_semantics=("arbitrary",) * n_axes, vmem_limit_bytes=VMEM_LIMIT)


def _rmsnorm_kernel(x_ref, g_ref, o_ref):
    x = x_ref[...]
    ms = jnp.mean(x * x, axis=-1, keepdims=True)
    o_ref[...] = (x * lax.rsqrt(ms + NORM_EPS) * g_ref[...]).astype(o_ref.dtype)


def rmsnorm(x, g, out_dtype, tr=256):
    m, d = x.shape
    return pl.pallas_call(
        _rmsnorm_kernel,
        grid=(m // tr,),
        in_specs=[pl.BlockSpec((tr, d), lambda i: (i, 0)),
                  pl.BlockSpec((1, d), lambda i: (0, 0))],
        out_specs=pl.BlockSpec((tr, d), lambda i: (i, 0)),
        out_shape=jax.ShapeDtypeStruct((m, d), out_dtype),
        compiler_params=_params(1),
        name="rmsnorm",
    )(x, g.reshape(1, d))


def _router_kernel(x_ref, g_ref, wr_ref, gate_ref, idx_ref):
    x = x_ref[...]
    ms = jnp.mean(x * x, axis=-1, keepdims=True)
    h = x * lax.rsqrt(ms + NORM_EPS) * g_ref[...]
    logits = jnp.dot(h, wr_ref[...], preferred_element_type=F32,
                     precision=lax.Precision.HIGHEST)
    lane = lax.broadcasted_iota(jnp.int32, logits.shape, 1)
    neg = jnp.float32(-jnp.inf)
    logits = jnp.where(lane < N_EXPERTS, logits, neg)
    v1 = jnp.max(logits, axis=-1, keepdims=True)
    i1 = jnp.min(jnp.where(logits == v1, lane, LANES), axis=-1, keepdims=True)
    rest = jnp.where(lane == i1, neg, logits)
    v2 = jnp.max(rest, axis=-1, keepdims=True)
    i2 = jnp.min(jnp.where(rest == v2, lane, LANES), axis=-1, keepdims=True)
    e2 = jnp.exp(v2 - v1)
    denom = 1.0 + e2
    g1 = 1.0 / denom
    g2 = e2 / denom
    gate_ref[...] = jnp.where(lane == 0, g1, jnp.where(lane == 1, g2, 0.0))
    idx_ref[...] = jnp.where(lane == 0, i1, jnp.where(lane == 1, i2, 0))


def router(x, g, w_router, tr=256):
    m, d = x.shape
    e = w_router.shape[1]
    wr = jnp.zeros((d, LANES), F32).at[:, :e].set(w_router)
    return pl.pallas_call(
        _router_kernel,
        grid=(m // tr,),
        in_specs=[pl.BlockSpec((tr, d), lambda i: (i, 0)),
                  pl.BlockSpec((1, d), lambda i: (0, 0)),
                  pl.BlockSpec((d, LANES), lambda i: (0, 0))],
        out_specs=[pl.BlockSpec((tr, LANES), lambda i: (i, 0)),
                   pl.BlockSpec((tr, LANES), lambda i: (i, 0))],
        out_shape=[jax.ShapeDtypeStruct((m, LANES), F32),
                   jax.ShapeDtypeStruct((m, LANES), jnp.int32)],
        compiler_params=_params(1),
        name="router",
    )(x, g.reshape(1, d), wr)


def _mm_kernel(*refs, n_parts, n_extra, n_scratch, epilogue):
    x_ref = refs[0]
    w_refs = refs[1:1 + n_parts]
    extra_refs = refs[1 + n_parts:1 + n_parts + n_extra]
    o_ref = refs[1 + n_parts + n_extra]
    wb_ref = refs[2 + n_parts + n_extra]
    scratch = refs[3 + n_parts + n_extra:]
    assert len(scratch) == n_scratch

    @pl.when(pl.program_id(1) == 0)
    def _cast_weights():
        for p in range(n_parts):
            wb_ref[p] = w_refs[p][...].astype(BF16)

    x = x_ref[...]
    accs = [jnp.dot(x, wb_ref[p], preferred_element_type=F32) for p in range(n_parts)]
    epilogue(accs, extra_refs, o_ref, scratch)


def matmul(x, w, *, w_lead, kb_x, kb_w, k, part_blocks, tn, tm, n_out, out_dtype,
           epilogue, extras=(), extra_specs=(), scratch_shapes=(), name):
    m = x.shape[0]
    n_parts = len(part_blocks)
    grid = (n_out // tn, m // tm)
    in_specs = [pl.BlockSpec((tm, k), lambda n, i: (i, kb_x))]
    for pb in part_blocks:
        in_specs.append(pl.BlockSpec((None, k, tn),
                                     lambda n, i, pb=pb: (w_lead, kb_w, pb + n)))
    in_specs.extend(extra_specs)
    kern = functools.partial(_mm_kernel, n_parts=n_parts, n_extra=len(extras),
                             n_scratch=len(scratch_shapes), epilogue=epilogue)
    return pl.pallas_call(
        kern,
        grid=grid,
        in_specs=in_specs,
        out_specs=pl.BlockSpec((tm, tn), lambda n, i: (i, n)),
        out_shape=jax.ShapeDtypeStruct((m, n_out), out_dtype),
        scratch_shapes=[pltpu.VMEM((n_parts, k, tn), BF16), *scratch_shapes],
        compiler_params=_params(2),
        name=name,
    )(x, *([w] * n_parts), *extras)


def _ep_store(accs, extra_refs, o_ref, scratch):
    o_ref[...] = accs[0].astype(o_ref.dtype)


def _ep_residual(accs, extra_refs, o_ref, scratch):
    o_ref[...] = extra_refs[0][...] + accs[0]


def _ep_swiglu(accs, extra_refs, o_ref, scratch):
    g, u = accs
    o_ref[...] = (g * jax.nn.sigmoid(g) * u).astype(o_ref.dtype)


def _ep_gelu(accs, extra_refs, o_ref, scratch):
    a = accs[0]
    o_ref[...] = (0.5 * a * (1.0 + lax.erf(a * SQRT_HALF))).astype(o_ref.dtype)


def _ep_conv(accs, extra_refs, o_ref, scratch, *, tiles_per_seq):
    b, c, xin = accs
    cw_ref, = extra_refs
    tail_ref, = scratch
    z = c * xin
    tm = z.shape[0]
    i = pl.program_id(1)

    @pl.when(i % tiles_per_seq == 0)
    def _zero_tail():
        tail_ref[...] = jnp.zeros_like(tail_ref)

    row = lax.broadcasted_iota(jnp.int32, z.shape, 0)
    t1 = tail_ref[7:8, :]
    t2 = tail_ref[6:7, :]
    z1 = jnp.where(row == 0, t1, pltpu.roll(z, 1, axis=0))
    z2 = jnp.where(row == 0, t2, jnp.where(row == 1, t1, pltpu.roll(z, 2, axis=0)))
    zc = cw_ref[0:1, :] * z2 + cw_ref[1:2, :] * z1 + cw_ref[2:3, :] * z
    o_ref[...] = (b * zc).astype(o_ref.dtype)
    tail_ref[...] = z[tm - 8:, :]


def _res_spec(tm, tn):
    return pl.BlockSpec((tm, tn), lambda n, i: (i, n))


def linear_residual(a, w, res, *, w_lead, kb_x=0, kb_w=0, k, name):
    n_out = w.shape[-1]
    tm, tn = TM, TN
    return matmul(a, w, w_lead=w_lead, kb_x=kb_x, kb_w=kb_w, k=k, part_blocks=(0,), tn=tn,
                  tm=tm, n_out=n_out, out_dtype=F32, epilogue=_ep_residual,
                  extras=(res,), extra_specs=(_res_spec(tm, tn),), name=name)


def short_conv_mixer(h, x, w_in, conv_w, w_out, j, seq):
    d = h.shape[1]
    tm, tn = min(TM, seq), TN_PART
    nb = d // tn
    cw = jnp.zeros((8, d), F32).at[:CONV_WIDTH].set(conv_w[j])
    gated = matmul(
        h, w_in, w_lead=j, kb_x=0, kb_w=0, k=d, part_blocks=(0, nb, 2 * nb), tn=tn, tm=tm,
        n_out=d, out_dtype=BF16,
        epilogue=functools.partial(_ep_conv, tiles_per_seq=seq // tm),
        extras=(cw,), extra_specs=(pl.BlockSpec((8, tn), lambda n, i: (0, n)),),
        scratch_shapes=(pltpu.VMEM((8, tn), F32),), name="conv_in")
    return linear_residual(gated, w_out, x, w_lead=j, k=d, name="conv_out")


def _gmlp_spatial_kernel(u_ref, v_ref, g_ref, b_ref, ws_ref, bst_ref, o_ref):
    v = v_ref[...]
    mu = jnp.mean(v, axis=-1, keepdims=True)
    vc = v - mu
    var = jnp.mean(vc * vc, axis=-1, keepdims=True)
    vn = (vc * lax.rsqrt(var + NORM_EPS) * g_ref[...] + b_ref[...]).astype(BF16)
    t_idx = lax.broadcasted_iota(jnp.int32, (CHUNK, CHUNK), 0)
    s_idx = lax.broadcasted_iota(jnp.int32, (CHUNK, CHUNK), 1)
    causal = s_idx <= t_idx
    n_groups = ws_ref.shape[0]
    for g in range(n_groups):
        sl = slice(g * GROUP_DIM, (g + 1) * GROUP_DIM)
        wm = jnp.where(causal, ws_ref[g], 0.0).astype(BF16)
        sv = jnp.dot(wm, vn[:, sl], preferred_element_type=F32) + bst_ref[:, g:g + 1]
        o_ref[:, sl] = (u_ref[:, sl] * sv).astype(o_ref.dtype)


def gmlp_spatial(uv, v_norm_g, v_norm_b, w_s, b_s):
    m = uv.shape[0]
    inner = uv.shape[1] // 2
    n_groups = inner // GROUP_DIM
    return pl.pallas_call(
        _gmlp_spatial_kernel,
        grid=(m // CHUNK,),
        in_specs=[pl.BlockSpec((CHUNK, inner), lambda i: (i, 0)),
                  pl.BlockSpec((CHUNK, inner), lambda i: (i, 1)),
                  pl.BlockSpec((1, inner), lambda i: (0, 0)),
                  pl.BlockSpec((1, inner), lambda i: (0, 0)),
                  pl.BlockSpec((n_groups, CHUNK, CHUNK), lambda i: (0, 0, 0)),
                  pl.BlockSpec((CHUNK, n_groups), lambda i: (0, 0))],
        out_specs=pl.BlockSpec((CHUNK, inner), lambda i: (i, 0)),
        out_shape=jax.ShapeDtypeStruct((m, inner), BF16),
        compiler_params=_params(1),
        name="gmlp_spatial",
    )(uv, uv, v_norm_g.reshape(1, inner), v_norm_b.reshape(1, inner), w_s, b_s.T)


def chunked_gmlp_mixer(h, x, w_in, v_norm_g, v_norm_b, w_s, b_s, w_out, j):
    d = h.shape[1]
    inner2 = w_in.shape[-1]
    uv = matmul(h, w_in, w_lead=j, kb_x=0, kb_w=0, k=d, part_blocks=(0,), tn=TN, tm=TM,
                n_out=inner2, out_dtype=F32, epilogue=_ep_gelu, name="gmlp_in")
    gated = gmlp_spatial(uv, v_norm_g[j], v_norm_b[j], w_s[j], b_s[j])
    return linear_residual(gated, w_out, x, w_lead=j, k=inner2 // 2, name="gmlp_out")


def _sb_attn_kernel(q_ref, k_ref, v_ref, o_ref, *, tq, scale):
    seq = q_ref.shape[0]
    n_heads = q_ref.shape[1] // HEAD_DIM
    n_blk = seq // tq
    r = lax.broadcasted_iota(jnp.int32, (tq, tq), 0)
    c = lax.broadcasted_iota(jnp.int32, (tq, tq), 1)
    strict = c < r
    suffix = (r >= c).astype(BF16)

    def head(hd):
        return slice(hd * HEAD_DIM, (hd + 1) * HEAD_DIM)

    def scores(qb, kb):
        z = lax.dot_general(qb, kb, (((1,), (1,)), ((), ())), preferred_element_type=F32)
        z = z * scale
        sp = jnp.maximum(z, 0.0) + jnp.log(1.0 + jnp.exp(-jnp.abs(z)))
        return z, sp

    def suffix_sum(sp):
        hi = sp.astype(BF16)
        lo = (sp - hi.astype(F32)).astype(BF16)
        return (jnp.dot(hi, suffix, preferred_element_type=F32)
                + jnp.dot(lo, suffix, preferred_element_type=F32))

    def q_block(qi, _):
        q0 = pl.multiple_of(qi * tq, tq)
        heads = range(n_heads)

        def block(k0, accs, carries, diagonal):
            zs, sps = zip(*[scores(q_ref[pl.ds(q0, tq), head(hd)],
                                   k_ref[pl.ds(k0, tq), head(hd)]) for hd in heads])
            if diagonal:
                sps = [jnp.where(strict, sp, 0.0) for sp in sps]
            sums = [suffix_sum(sp) for sp in sps]
            if diagonal:
                atts = [jnp.where(strict, jnp.exp(z - s), 0.0) for z, s in zip(zs, sums)]
            else:
                atts = [jnp.exp(z - s - c) for z, s, c in zip(zs, sums, carries)]
            pvs = [jnp.dot(att.astype(BF16), v_ref[pl.ds(k0, tq), head(hd)],
                           preferred_element_type=F32) for hd, att in zip(heads, atts)]
            totals = [jnp.sum(sp, axis=-1, keepdims=True) for sp in sps]
            if diagonal:
                return pvs, totals
            return ([a + pv for a, pv in zip(accs, pvs)],
                    [c + t for c, t in zip(carries, totals)])

        accs, carries = block(q0, None, None, True)

        def kv_block(jj, state):
            k0 = pl.multiple_of((qi - 1 - jj) * tq, tq)
            accs, carries = block(k0, state[:n_heads], state[n_heads:], False)
            return tuple(accs) + tuple(carries)

        state = lax.fori_loop(0, qi, kv_block, tuple(accs) + tuple(carries))
        for hd in heads:
            o_ref[pl.ds(q0, tq), head(hd)] = state[hd].astype(o_ref.dtype)
        return 0

    lax.fori_loop(0, n_blk, q_block, 0)


def sb_attention(qkv, batch, seq):
    m, three_d = qkv.shape
    n_heads = three_d // (3 * HEAD_DIM)
    hb = min(SB_HEADS_PER_STEP, n_heads)
    n_groups = n_heads // hb
    width = hb * HEAD_DIM
    kern = functools.partial(_sb_attn_kernel, tq=TQ, scale=HEAD_DIM ** -0.5)
    return pl.pallas_call(
        kern,
        grid=(batch, n_groups),
        in_specs=[pl.BlockSpec((seq, width), lambda b, h: (b, h)),
                  pl.BlockSpec((seq, width), lambda b, h: (b, n_groups + h)),
                  pl.BlockSpec((seq, width), lambda b, h: (b, 2 * n_groups + h))],
        out_specs=pl.BlockSpec((seq, width), lambda b, h: (b, h)),
        out_shape=jax.ShapeDtypeStruct((m, n_heads * HEAD_DIM), BF16),
        compiler_params=_params(2),
        name="sb_attention",
    )(qkv, qkv, qkv)


def stick_breaking_mixer(h, x, w_qkv, w_out, j, batch, seq):
    d = h.shape[1]
    qkv = matmul(h, w_qkv, w_lead=j, kb_x=0, kb_w=0, k=d, part_blocks=(0,), tn=TN, tm=TM,
                 n_out=w_qkv.shape[-1], out_dtype=BF16, epilogue=_ep_store, name="sb_qkv")
    o = sb_attention(qkv, batch, seq)
    return linear_residual(o, w_out, x, w_lead=j, k=w_out.shape[1], name="sb_out")


def swiglu_hidden(h, w_gu, w_lead, name):
    d = h.shape[1]
    f = w_gu.shape[-1] // 2
    tm, tn = TM, TN_PART
    return matmul(h, w_gu, w_lead=w_lead, kb_x=0, kb_w=0, k=d, part_blocks=(0, f // tn),
                  tn=tn, tm=tm, n_out=f, out_dtype=BF16, epilogue=_ep_swiglu, name=name)


def dense_swiglu(h, x, w_gu, w_down, f_idx):
    k_split = w_gu.shape[1]
    a = swiglu_hidden(h, w_gu, f_idx, "dense_gu")
    for kb in range(w_down.shape[1] // k_split):
        x = linear_residual(a, w_down, x, w_lead=f_idx, kb_x=kb, kb_w=kb, k=k_split,
                            name="dense_down")
    return x


def moe_routing(idx2, n_exp, tmg, n_col_tiles):
    m = idx2.shape[0]
    n_slots = 2 * m
    t_bound = n_slots // tmg + n_exp
    flat = idx2.reshape(-1)
    onehot = (flat[:, None] == jnp.arange(n_exp, dtype=jnp.int32)[None, :]).astype(jnp.int32)
    rank = jnp.sum((jnp.cumsum(onehot, axis=0) - onehot) * onehot, axis=1)
    counts = jnp.sum(onehot, axis=0)
    tiles = (counts + tmg - 1) // tmg
    tile_end = jnp.cumsum(tiles)
    tile_start = tile_end - tiles
    used = tile_end[-1]
    pos = tile_start[flat] * tmg + rank
    row_token = jnp.zeros((t_bound * tmg,), jnp.int32).at[pos].set(
        jnp.arange(n_slots, dtype=jnp.int32) // 2)

    def work_list(nt):
        item_end = jnp.cumsum(tiles * nt)
        total = item_end[-1]
        w = jnp.arange(nt * t_bound, dtype=jnp.int32)
        valid = w < total
        wc = jnp.minimum(w, total - 1)
        e = jnp.sum((item_end[None, :] <= wc[:, None]).astype(jnp.int32), axis=1)
        local = wc - (item_end - tiles * nt)[e]
        r = jnp.maximum(tiles[e], 1)
        n = local // r
        row = tile_start[e] + local % r
        first = valid & (local % r == 0)
        k = w - total
        out_row = jnp.where(valid, row, used + k // nt)
        out_col = jnp.where(valid, n, k % nt)
        return jnp.stack([e, n, row, out_row, out_col, valid.astype(jnp.int32),
                          first.astype(jnp.int32)]).astype(jnp.int32)

    return pos, row_token, [work_list(nt) for nt in n_col_tiles]


def _issue_row_gather(n_rows, src_index, src_hbm, dst_ref, sem):
    def body(r, _):
        pltpu.make_async_copy(src_hbm.at[pl.ds(src_index(r), 1)],
                              dst_ref.at[pl.ds(r, 1)], sem).start()
        return 0
    lax.fori_loop(0, n_rows, body, 0)


def _wait_row_gather(n_rows, src_hbm, dst_ref, sem):
    def body(r, _):
        pltpu.make_async_copy(src_hbm.at[pl.ds(0, 1)], dst_ref.at[pl.ds(r, 1)], sem).wait()
        return 0
    lax.fori_loop(0, n_rows, body, 0)


def _gather_norm_kernel(tok_ref, x_hbm, g_ref, o_ref, buf, sems, *, tg):
    i = pl.program_id(0)
    slot = i % 2

    def issue(step, s):
        _issue_row_gather(tg, lambda r: tok_ref[step * tg + r], x_hbm, buf.at[s], sems.at[s])

    @pl.when(i == 0)
    def _prime():
        issue(0, 0)

    @pl.when(i + 1 < pl.num_programs(0))
    def _prefetch():
        issue(i + 1, 1 - slot)

    _wait_row_gather(tg, x_hbm, buf.at[slot], sems.at[slot])
    x = buf[slot]
    ms = jnp.mean(x * x, axis=-1, keepdims=True)
    o_ref[...] = (x * lax.rsqrt(ms + NORM_EPS) * g_ref[...]).astype(o_ref.dtype)


def gather_norm(x, g, row_token, tg=256):
    m, d = x.shape
    p = row_token.shape[0]
    grid_spec = pltpu.PrefetchScalarGridSpec(
        num_scalar_prefetch=1,
        grid=(p // tg,),
        in_specs=[pl.BlockSpec(memory_space=pl.ANY),
                  pl.BlockSpec((1, d), lambda i, tok: (0, 0))],
        out_specs=pl.BlockSpec((tg, d), lambda i, tok: (i, 0)),
        scratch_shapes=[pltpu.VMEM((2, tg, d), F32), pltpu.SemaphoreType.DMA((2,))])
    return pl.pallas_call(
        functools.partial(_gather_norm_kernel, tg=tg),
        grid_spec=grid_spec,
        out_shape=jax.ShapeDtypeStruct((p, d), BF16),
        compiler_params=_params(1),
        name="moe_gather_norm",
    )(row_token, x, g.reshape(1, d))


def _grouped_mm_kernel(work_ref, x_ref, *refs, n_parts, epilogue):
    w_refs = refs[:n_parts]
    o_ref = refs[n_parts]
    wb_ref = refs[n_parts + 1]
    w = pl.program_id(0)

    @pl.when(work_ref[6, w] == 1)
    def _cast_weights():
        for p in range(n_parts):
            wb_ref[p] = w_refs[p][...].astype(BF16)

    @pl.when(work_ref[5, w] == 1)
    def _compute():
        x = x_ref[...]
        accs = [jnp.dot(x, wb_ref[p], preferred_element_type=F32) for p in range(n_parts)]
        epilogue(accs, (), o_ref, ())

    @pl.when(work_ref[5, w] == 0)
    def _zero_fill():
        o_ref[...] = jnp.zeros_like(o_ref)


def grouped_matmul(x, w, work, *, lead_base, k, part_blocks, tn, tmg, n_out, out_dtype,
                   epilogue, name):
    p = x.shape[0]
    n_parts = len(part_blocks)
    in_specs = [pl.BlockSpec((tmg, k), lambda i, wk: (wk[2, i], 0))]
    for pb in part_blocks:
        in_specs.append(pl.BlockSpec(
            (None, k, tn), lambda i, wk, pb=pb: (lead_base + wk[0, i], 0, pb + wk[1, i])))
    grid_spec = pltpu.PrefetchScalarGridSpec(
        num_scalar_prefetch=1,
        grid=(work.shape[1],),
        in_specs=in_specs,
        out_specs=pl.BlockSpec((tmg, tn), lambda i, wk: (wk[3, i], wk[4, i])),
        scratch_shapes=[pltpu.VMEM((n_parts, k, tn), BF16)])
    return pl.pallas_call(
        functools.partial(_grouped_mm_kernel, n_parts=n_parts, epilogue=epilogue),
        grid_spec=grid_spec,
        out_shape=jax.ShapeDtypeStruct((p, n_out), out_dtype),
        compiler_params=_params(1),
        name=name,
    )(work, x, *([w] * n_parts))


def _combine_kernel(pos_ref, x_ref, gate_ref, y_hbm, o_ref, buf, sems, *, tc):
    i = pl.program_id(0)
    slot = i % 2

    def issue(step, s):
        for k in range(2):
            _issue_row_gather(tc, lambda r, k=k: pos_ref[2 * (step * tc + r) + k], y_hbm,
                              buf.at[s, k], sems.at[s, k])

    @pl.when(i == 0)
    def _prime():
        issue(0, 0)

    @pl.when(i + 1 < pl.num_programs(0))
    def _prefetch():
        issue(i + 1, 1 - slot)

    for k in range(2):
        _wait_row_gather(tc, y_hbm, buf.at[slot, k], sems.at[slot, k])
    gates = gate_ref[...]
    o_ref[...] = x_ref[...] + gates[:, 0:1] * buf[slot, 0] + gates[:, 1:2] * buf[slot, 1]


def moe_combine(x, gates, y, pos, tc=256):
    m, d = x.shape
    grid_spec = pltpu.PrefetchScalarGridSpec(
        num_scalar_prefetch=1,
        grid=(m // tc,),
        in_specs=[pl.BlockSpec((tc, d), lambda i, pos: (i, 0)),
                  pl.BlockSpec((tc, LANES), lambda i, pos: (i, 0)),
                  pl.BlockSpec(memory_space=pl.ANY)],
        out_specs=pl.BlockSpec((tc, d), lambda i, pos: (i, 0)),
        scratch_shapes=[pltpu.VMEM((2, 2, tc, d), F32), pltpu.SemaphoreType.DMA((2, 2))])
    return pl.pallas_call(
        functools.partial(_combine_kernel, tc=tc),
        grid_spec=grid_spec,
        out_shape=jax.ShapeDtypeStruct((m, d), F32),
        compiler_params=_params(1),
        name="moe_combine",
    )(pos, x, gates, y)


def moe_swiglu(x, norm_g, w_router, w_gu, w_down, f_idx):
    n_exp, d, two_f = w_gu.shape[1:]
    d_exp = two_f // 2
    tmg, tn_gu, tn_down = TMG, TN, 2 * TN
    gates, idx = router(x, norm_g, w_router)
    pos, row_token, (work_gu, work_down) = moe_routing(
        idx[:, :2], n_exp, tmg, (d_exp // tn_gu, d // tn_down))
    hs = gather_norm(x, norm_g, row_token)
    act = grouped_matmul(hs, w_gu.reshape(-1, d, two_f), work_gu, lead_base=f_idx * n_exp, k=d,
                         part_blocks=(0, d_exp // tn_gu), tn=tn_gu, tmg=tmg, n_out=d_exp,
                         out_dtype=BF16, epilogue=_ep_swiglu, name="moe_gu")
    y = grouped_matmul(act, w_down.reshape(-1, d_exp, d), work_down, lead_base=f_idx * n_exp,
                       k=d_exp, part_blocks=(0,), tn=tn_down, tmg=tmg, n_out=d, out_dtype=F32,
                       epilogue=_ep_store, name="moe_down")
    return moe_combine(x, gates, y, pos)


def kernel(x, mix_norm, ffn_norm, conv_w_in, conv_w, conv_w_out, gmlp_w_in, gmlp_v_norm_g,
           gmlp_v_norm_b, gmlp_w_s, gmlp_b_s, gmlp_w_out, sb_w_qkv, sb_w_out, dense_w_gu,
           dense_w_down, moe_w_router, moe_w_gu, moe_w_down, final_norm):
    batch, seq, d = x.shape
    depth = mix_norm.shape[0]
    x = x.reshape(batch * seq, d)
    for i in range(depth):
        h = rmsnorm(x, mix_norm[i], BF16)
        kind, j = i % 3, i // 3
        if kind == 0:
            x = short_conv_mixer(h, x, conv_w_in, conv_w, conv_w_out, j, seq)
        elif kind == 1:
            x = chunked_gmlp_mixer(h, x, gmlp_w_in, gmlp_v_norm_g, gmlp_v_norm_b, gmlp_w_s,
                                   gmlp_b_s, gmlp_w_out, j)
        else:
            x = stick_breaking_mixer(h, x, sb_w_qkv, sb_w_out, j, batch, seq)
        f = i // 2
        if i % 2 == 0:
            h = rmsnorm(x, ffn_norm[i], BF16)
            x = dense_swiglu(h, x, dense_w_gu, dense_w_down, f)
        else:
            x = moe_swiglu(x, ffn_norm[i], moe_w_router[f], moe_w_gu, moe_w_down, f)
    return rmsnorm(x, final_norm, F32).reshape(batch, seq, d)
```

```python
import functools

import jax
import jax.numpy as jnp
from jax import lax
from jax.experimental import pallas as pl
from jax.experimental.pallas import tpu as pltpu

F32 = jnp.float32
BF16 = jnp.bfloat16

NORM_EPS = 1e-5
CONV_WIDTH = 3
CHUNK = 128
GROUP_DIM = 128
HEAD_DIM = 128
N_EXPERTS = 8
LANES = 128
VMEM_LIMIT = 56 * 1024 * 1024
SQRT_HALF = 0.7071067811865476
LOG2_E = 1.4426950408889634

TM = 1024
TN = 512
TN_PART = 256
TQ = 256
SB_HEADS_PER_STEP = 4
TMG = 512


def _params(n_axes):
    return pltpu.CompilerParams(
        dimension_semantics=("arbitrary",) * n_axes, vmem_limit_bytes=VMEM_LIMIT)


def _rmsnorm_kernel(x_ref, g_ref, o_ref):
    x = x_ref[...]
    ms = jnp.mean(x * x, axis=-1, keepdims=True)
    o_ref[...] = (x * lax.rsqrt(ms + NORM_EPS) * g_ref[...]).astype(o_ref.dtype)


def rmsnorm(x, g, out_dtype, tr=256):
    m, d = x.shape
    return pl.pallas_call(
        _rmsnorm_kernel,
        grid=(m // tr,),
        in_specs=[pl.BlockSpec((tr, d), lambda i: (i, 0)),
                  pl.BlockSpec((1, d), lambda i: (0, 0))],
        out_specs=pl.BlockSpec((tr, d), lambda i: (i, 0)),
        out_shape=jax.ShapeDtypeStruct((m, d), out_dtype),
        compiler_params=_params(1),
        name="rmsnorm",
    )(x, g.reshape(1, d))


def _router_kernel(x_ref, g_ref, wr_ref, gate_ref, idx_ref):
    x = x_ref[...]
    ms = jnp.mean(x * x, axis=-1, keepdims=True)
    h = x * lax.rsqrt(ms + NORM_EPS) * g_ref[...]
    logits = jnp.dot(h, wr_ref[...], preferred_element_type=F32,
                     precision=lax.Precision.HIGHEST)
    lane = lax.broadcasted_iota(jnp.int32, logits.shape, 1)
    neg = jnp.float32(-jnp.inf)
    logits = jnp.where(lane < N_EXPERTS, logits, neg)
    v1 = jnp.max(logits, axis=-1, keepdims=True)
    i1 = jnp.min(jnp.where(logits == v1, lane, LANES), axis=-1, keepdims=True)
    rest = jnp.where(lane == i1, neg, logits)
    v2 = jnp.max(rest, axis=-1, keepdims=True)
    i2 = jnp.min(jnp.where(rest == v2, lane, LANES), axis=-1, keepdims=True)
    e2 = jnp.exp(v2 - v1)
    denom = 1.0 + e2
    g1 = 1.0 / denom
    g2 = e2 / denom
    gate_ref[...] = jnp.where(lane == 0, g1, jnp.where(lane == 1, g2, 0.0))
    idx_ref[...] = jnp.where(lane == 0, i1, jnp.where(lane == 1, i2, 0))


def router(x, g, w_router, tr=256):
    m, d = x.shape
    e = w_router.shape[1]
    wr = jnp.zeros((d, LANES), F32).at[:, :e].set(w_router)
    return pl.pallas_call(
        _router_kernel,
        grid=(m // tr,),
        in_specs=[pl.BlockSpec((tr, d), lambda i: (i, 0)),
                  pl.BlockSpec((1, d), lambda i: (0, 0)),
                  pl.BlockSpec((d, LANES), lambda i: (0, 0))],
        out_specs=[pl.BlockSpec((tr, LANES), lambda i: (i, 0)),
                   pl.BlockSpec((tr, LANES), lambda i: (i, 0))],
        out_shape=[jax.ShapeDtypeStruct((m, LANES), F32),
                   jax.ShapeDtypeStruct((m, LANES), jnp.int32)],
        compiler_params=_params(1),
        name="router",
    )(x, g.reshape(1, d), wr)


def _mm_kernel(*refs, n_parts, n_extra, n_scratch, epilogue):
    x_ref = refs[0]
    w_refs = refs[1:1 + n_parts]
    extra_refs = refs[1 + n_parts:1 + n_parts + n_extra]
    o_ref = refs[1 + n_parts + n_extra]
    wb_ref = refs[2 + n_parts + n_extra]
    scratch = refs[3 + n_parts + n_extra:]
    assert len(scratch) == n_scratch

    @pl.when(pl.program_id(1) == 0)
    def _cast_weights():
        for p in range(n_parts):
            wb_ref[p] = w_refs[p][...].astype(BF16)

    x = x_ref[...]
    accs = [jnp.dot(x, wb_ref[p], preferred_element_type=F32) for p in range(n_parts)]
    epilogue(accs, extra_refs, o_ref, scratch)


def matmul(x, w, *, w_lead, kb_x, kb_w, k, part_blocks, tn, tm, n_out, out_dtype,
           epilogue, extras=(), extra_specs=(), scratch_shapes=(), name):
    m = x.shape[0]
    n_parts = len(part_blocks)
    grid = (n_out // tn, m // tm)
    in_specs = [pl.BlockSpec((tm, k), lambda n, i: (i, kb_x))]
    for pb in part_blocks:
        in_specs.append(pl.BlockSpec((None, k, tn),
                                     lambda n, i, pb=pb: (w_lead, kb_w, pb + n)))
    in_specs.extend(extra_specs)
    kern = functools.partial(_mm_kernel, n_parts=n_parts, n_extra=len(extras),
                             n_scratch=len(scratch_shapes), epilogue=epilogue)
    return pl.pallas_call(
        kern,
        grid=grid,
        in_specs=in_specs,
        out_specs=pl.BlockSpec((tm, tn), lambda n, i: (i, n)),
        out_shape=jax.ShapeDtypeStruct((m, n_out), out_dtype),
        scratch_shapes=[pltpu.VMEM((n_parts, k, tn), BF16), *scratch_shapes],
        compiler_params=_params(2),
        name=name,
    )(x, *([w] * n_parts), *extras)


def _ep_store(accs, extra_refs, o_ref, scratch):
    o_ref[...] = accs[0].astype(o_ref.dtype)


def _ep_residual(accs, extra_refs, o_ref, scratch):
    o_ref[...] = extra_refs[0][...] + accs[0]


def _ep_swiglu(accs, extra_refs, o_ref, scratch):
    g, u = accs
    o_ref[...] = (g * jax.nn.sigmoid(g) * u).astype(o_ref.dtype)


def _ep_gelu(accs, extra_refs, o_ref, scratch):
    a = accs[0]
    o_ref[...] = (0.5 * a * (1.0 + lax.erf(a * SQRT_HALF))).astype(o_ref.dtype)


def _ep_conv(accs, extra_refs, o_ref, scratch, *, tiles_per_seq):
    b, c, xin = accs
    cw_ref, = extra_refs
    tail_ref, = scratch
    z = c * xin
    tm = z.shape[0]
    i = pl.program_id(1)

    @pl.when(i % tiles_per_seq == 0)
    def _zero_tail():
        tail_ref[...] = jnp.zeros_like(tail_ref)

    row = lax.broadcasted_iota(jnp.int32, z.shape, 0)
    t1 = tail_ref[7:8, :]
    t2 = tail_ref[6:7, :]
    z1 = jnp.where(row == 0, t1, pltpu.roll(z, 1, axis=0))
    z2 = jnp.where(row == 0, t2, jnp.where(row == 1, t1, pltpu.roll(z, 2, axis=0)))
    zc = cw_ref[0:1, :] * z2 + cw_ref[1:2, :] * z1 + cw_ref[2:3, :] * z
    o_ref[...] = (b * zc).astype(o_ref.dtype)
    tail_ref[...] = z[tm - 8:, :]


def _res_spec(tm, tn):
    return pl.BlockSpec((tm, tn), lambda n, i: (i, n))


def linear_residual(a, w, res, *, w_lead, kb_x=0, kb_w=0, k, name):
    n_out = w.shape[-1]
    tm, tn = TM, TN
    return matmul(a, w, w_lead=w_lead, kb_x=kb_x, kb_w=kb_w, k=k, part_blocks=(0,), tn=tn,
                  tm=tm, n_out=n_out, out_dtype=F32, epilogue=_ep_residual,
                  extras=(res,), extra_specs=(_res_spec(tm, tn),), name=name)


def short_conv_mixer(h, x, w_in, conv_w, w_out, j, seq):
    d = h.shape[1]
    tm, tn = min(TM, seq), TN_PART
    nb = d // tn
    cw = jnp.zeros((8, d), F32).at[:CONV_WIDTH].set(conv_w[j])
    gated = matmul(
        h, w_in, w_lead=j, kb_x=0, kb_w=0, k=d, part_blocks=(0, nb, 2 * nb), tn=tn, tm=tm,
        n_out=d, out_dtype=BF16,
        epilogue=functools.partial(_ep_conv, tiles_per_seq=seq // tm),
        extras=(cw,), extra_specs=(pl.BlockSpec((8, tn), lambda n, i: (0, n)),),
        scratch_shapes=(pltpu.VMEM((8, tn), F32),), name="conv_in")
    return linear_residual(gated, w_out, x, w_lead=j, k=d, name="conv_out")


def _gmlp_spatial_kernel(u_ref, v_ref, g_ref, b_ref, ws_ref, bst_ref, o_ref):
    v = v_ref[...]
    mu = jnp.mean(v, axis=-1, keepdims=True)
    vc = v - mu
    var = jnp.mean(vc * vc, axis=-1, keepdims=True)
    vn = (vc * lax.rsqrt(var + NORM_EPS) * g_ref[...] + b_ref[...]).astype(BF16)
    t_idx = lax.broadcasted_iota(jnp.int32, (CHUNK, CHUNK), 0)
    s_idx = lax.broadcasted_iota(jnp.int32, (CHUNK, CHUNK), 1)
    causal = s_idx <= t_idx
    n_groups = ws_ref.shape[0]
    for g in range(n_groups):
        sl = slice(g * GROUP_DIM, (g + 1) * GROUP_DIM)
        wm = jnp.where(causal, ws_ref[g], 0.0).astype(BF16)
        sv = jnp.dot(wm, vn[:, sl], preferred_element_type=F32) + bst_ref[:, g:g + 1]
        o_ref[:, sl] = (u_ref[:, sl] * sv).astype(o_ref.dtype)


def gmlp_spatial(uv, v_norm_g, v_norm_b, w_s, b_s):
    m = uv.shape[0]
    inner = uv.shape[1] // 2
    n_groups = inner // GROUP_DIM
    return pl.pallas_call(
        _gmlp_spatial_kernel,
        grid=(m // CHUNK,),
        in_specs=[pl.BlockSpec((CHUNK, inner), lambda i: (i, 0)),
                  pl.BlockSpec((CHUNK, inner), lambda i: (i, 1)),
                  pl.BlockSpec((1, inner), lambda i: (0, 0)),
                  pl.BlockSpec((1, inner), lambda i: (0, 0)),
                  pl.BlockSpec((n_groups, CHUNK, CHUNK), lambda i: (0, 0, 0)),
                  pl.BlockSpec((CHUNK, n_groups), lambda i: (0, 0))],
        out_specs=pl.BlockSpec((CHUNK, inner), lambda i: (i, 0)),
        out_shape=jax.ShapeDtypeStruct((m, inner), BF16),
        compiler_params=_params(1),
        name="gmlp_spatial",
    )(uv, uv, v_norm_g.reshape(1, inner), v_norm_b.reshape(1, inner), w_s, b_s.T)


def chunked_gmlp_mixer(h, x, w_in, v_norm_g, v_norm_b, w_s, b_s, w_out, j):
    d = h.shape[1]
    inner2 = w_in.shape[-1]
    uv = matmul(h, w_in, w_lead=j, kb_x=0, kb_w=0, k=d, part_blocks=(0,), tn=TN, tm=TM,
                n_out=inner2, out_dtype=F32, epilogue=_ep_gelu, name="gmlp_in")
    gated = gmlp_spatial(uv, v_norm_g[j], v_norm_b[j], w_s[j], b_s[j])
    return linear_residual(gated, w_out, x, w_lead=j, k=inner2 // 2, name="gmlp_out")


def _sb_attn_kernel(q_ref, k_ref, v_ref, o_ref, *, tq, scale):
    seq = q_ref.shape[0]
    n_heads = q_ref.shape[1] // HEAD_DIM
    n_blk = seq // tq
    r = lax.broadcasted_iota(jnp.int32, (tq, tq), 0)
    c = lax.broadcasted_iota(jnp.int32, (tq, tq), 1)
    strict = c < r
    suffix = (r >= c).astype(BF16)

    def head(hd):
        return slice(hd * HEAD_DIM, (hd + 1) * HEAD_DIM)

    def scores(qb, kb):
        z2 = lax.dot_general(qb, kb, (((1,), (1,)), ((), ())), preferred_element_type=F32)
        z2 = z2 * (scale * LOG2_E)
        sp = jnp.maximum(z2, 0.0) + jnp.log2(1.0 + jnp.exp2(-jnp.abs(z2)))
        return z2, sp

    def suffix_sum(sp):
        hi = sp.astype(BF16)
        lo = (sp - hi.astype(F32)).astype(BF16)
        return (jnp.dot(hi, suffix, preferred_element_type=F32)
                + jnp.dot(lo, suffix, preferred_element_type=F32))

    def q_block(qi, _):
        q0 = pl.multiple_of(qi * tq, tq)
        heads = range(n_heads)

        def block(k0, accs, carries, diagonal):
            zs, sps, atts, pvs, totals = {}, {}, {}, {}, {}

            def stage_scores(hd):
                z, sp = scores(q_ref[pl.ds(q0, tq), head(hd)], k_ref[pl.ds(k0, tq), head(hd)])
                zs[hd], sps[hd] = z, (jnp.where(strict, sp, 0.0) if diagonal else sp)

            def stage_weights(hd):
                s = suffix_sum(sps[hd])
                if diagonal:
                    atts[hd] = jnp.where(strict, jnp.exp2(zs[hd] - s), 0.0)
                else:
                    atts[hd] = jnp.exp2(zs[hd] - s - carries[hd])
                totals[hd] = s[:, 0:1]

            def stage_values(hd):
                pvs[hd] = jnp.dot(atts[hd].astype(BF16), v_ref[pl.ds(k0, tq), head(hd)],
                                  preferred_element_type=F32)

            for stage in (stage_scores, stage_weights, stage_values):
                for hd in heads:
                    stage(hd)
            if diagonal:
                return [pvs[hd] for hd in heads], [totals[hd] for hd in heads]
            return ([accs[hd] + pvs[hd] for hd in heads],
                    [carries[hd] + totals[hd] for hd in heads])

        accs, carries = block(q0, None, None, True)

        def kv_block(jj, state):
            k0 = pl.multiple_of((qi - 1 - jj) * tq, tq)
            accs, carries = block(k0, state[:n_heads], state[n_heads:], False)
            return tuple(accs) + tuple(carries)

        state = lax.fori_loop(0, qi, kv_block, tuple(accs) + tuple(carries))
        for hd in heads:
            o_ref[pl.ds(q0, tq), head(hd)] = state[hd].astype(o_ref.dtype)
        return 0

    lax.fori_loop(0, n_blk, q_block, 0)


def sb_attention(qkv, batch, seq):
    m, three_d = qkv.shape
    n_heads = three_d // (3 * HEAD_DIM)
    hb = min(SB_HEADS_PER_STEP, n_heads)
    n_groups = n_heads // hb
    width = hb * HEAD_DIM
    kern = functools.partial(_sb_attn_kernel, tq=TQ, scale=HEAD_DIM ** -0.5)
    return pl.pallas_call(
        kern,
        grid=(batch, n_groups),
        in_specs=[pl.BlockSpec((seq, width), lambda b, h: (b, h)),
                  pl.BlockSpec((seq, width), lambda b, h: (b, n_groups + h)),
                  pl.BlockSpec((seq, width), lambda b, h: (b, 2 * n_groups + h))],
        out_specs=pl.BlockSpec((seq, width), lambda b, h: (b, h)),
        out_shape=jax.ShapeDtypeStruct((m, n_heads * HEAD_DIM), BF16),
        compiler_params=_params(2),
        name="sb_attention",
    )(qkv, qkv, qkv)


def stick_breaking_mixer(h, x, w_qkv, w_out, j, batch, seq):
    d = h.shape[1]
    qkv = matmul(h, w_qkv, w_lead=j, kb_x=0, kb_w=0, k=d, part_blocks=(0,), tn=TN, tm=TM,
                 n_out=w_qkv.shape[-1], out_dtype=BF16, epilogue=_ep_store, name="sb_qkv")
    o = sb_attention(qkv, batch, seq)
    return linear_residual(o, w_out, x, w_lead=j, k=w_out.shape[1], name="sb_out")


def swiglu_hidden(h, w_gu, w_lead, name):
    d = h.shape[1]
    f = w_gu.shape[-1] // 2
    tm, tn = TM, TN_PART
    return matmul(h, w_gu, w_lead=w_lead, kb_x=0, kb_w=0, k=d, part_blocks=(0, f // tn),
                  tn=tn, tm=tm, n_out=f, out_dtype=BF16, epilogue=_ep_swiglu, name=name)


def dense_swiglu(h, x, w_gu, w_down, f_idx):
    k_split = w_gu.shape[1]
    a = swiglu_hidden(h, w_gu, f_idx, "dense_gu")
    for kb in range(w_down.shape[1] // k_split):
        x = linear_residual(a, w_down, x, w_lead=f_idx, kb_x=kb, kb_w=kb, k=k_split,
                            name="dense_down")
    return x


def moe_routing(idx2, n_exp, tmg, n_col_tiles):
    m = idx2.shape[0]
    n_slots = 2 * m
    t_bound = n_slots // tmg + n_exp
    flat = idx2.reshape(-1)
    onehot = (flat[:, None] == jnp.arange(n_exp, dtype=jnp.int32)[None, :]).astype(jnp.int32)
    rank = jnp.sum((jnp.cumsum(onehot, axis=0) - onehot) * onehot, axis=1)
    counts = jnp.sum(onehot, axis=0)
    tiles = (counts + tmg - 1) // tmg
    tile_end = jnp.cumsum(tiles)
    tile_start = tile_end - tiles
    used = tile_end[-1]
    pos = tile_start[flat] * tmg + rank
    row_token = jnp.zeros((t_bound * tmg,), jnp.int32).at[pos].set(
        jnp.arange(n_slots, dtype=jnp.int32) // 2)

    def work_list(nt):
        item_end = jnp.cumsum(tiles * nt)
        total = item_end[-1]
        w = jnp.arange(nt * t_bound, dtype=jnp.int32)
        valid = w < total
        wc = jnp.minimum(w, total - 1)
        e = jnp.sum((item_end[None, :] <= wc[:, None]).astype(jnp.int32), axis=1)
        local = wc - (item_end - tiles * nt)[e]
        r = jnp.maximum(tiles[e], 1)
        n = local // r
        row = tile_start[e] + local % r
        first = valid & (local % r == 0)
        k = w - total
        out_row = jnp.where(valid, row, used + k // nt)
        out_col = jnp.where(valid, n, k % nt)
        return jnp.stack([e, n, row, out_row, out_col, valid.astype(jnp.int32),
                          first.astype(jnp.int32)]).astype(jnp.int32)

    return pos, row_token, [work_list(nt) for nt in n_col_tiles]


def _issue_row_gather(n_rows, src_index, src_hbm, dst_ref, sem):
    def body(pair, _):
        for prio in range(2):
            r = 2 * pair + prio
            pltpu.make_async_copy(src_hbm.at[pl.ds(src_index(r), 1)],
                                  dst_ref.at[pl.ds(r, 1)], sem).start(priority=prio)
        return 0
    lax.fori_loop(0, n_rows // 2, body, 0)


def _wait_row_gather(n_rows, src_hbm, dst_ref, sem):
    def body(r, _):
        pltpu.make_async_copy(src_hbm.at[pl.ds(0, 1)], dst_ref.at[pl.ds(r, 1)], sem).wait()
        return 0
    lax.fori_loop(0, n_rows, body, 0)


def _gather_norm_kernel(tok_ref, x_hbm, g_ref, o_ref, buf, sems, *, tg):
    i = pl.program_id(0)
    slot = i % 2

    def issue(step, s):
        _issue_row_gather(tg, lambda r: tok_ref[step * tg + r], x_hbm, buf.at[s], sems.at[s])

    @pl.when(i == 0)
    def _prime():
        issue(0, 0)

    @pl.when(i + 1 < pl.num_programs(0))
    def _prefetch():
        issue(i + 1, 1 - slot)

    _wait_row_gather(tg, x_hbm, buf.at[slot], sems.at[slot])
    x = buf[slot]
    ms = jnp.mean(x * x, axis=-1, keepdims=True)
    o_ref[...] = (x * lax.rsqrt(ms + NORM_EPS) * g_ref[...]).astype(o_ref.dtype)


def gather_norm(x, g, row_token, tg=256):
    m, d = x.shape
    p = row_token.shape[0]
    grid_spec = pltpu.PrefetchScalarGridSpec(
        num_scalar_prefetch=1,
        grid=(p // tg,),
        in_specs=[pl.BlockSpec(memory_space=pl.ANY),
                  pl.BlockSpec((1, d), lambda i, tok: (0, 0))],
        out_specs=pl.BlockSpec((tg, d), lambda i, tok: (i, 0)),
        scratch_shapes=[pltpu.VMEM((2, tg, d), F32), pltpu.SemaphoreType.DMA((2,))])
    return pl.pallas_call(
        functools.partial(_gather_norm_kernel, tg=tg),
        grid_spec=grid_spec,
        out_shape=jax.ShapeDtypeStruct((p, d), BF16),
        compiler_params=_params(1),
        name="moe_gather_norm",
    )(row_token, x, g.reshape(1, d))


def _grouped_mm_kernel(work_ref, x_ref, *refs, n_parts, epilogue):
    w_refs = refs[:n_parts]
    o_ref = refs[n_parts]
    wb_ref = refs[n_parts + 1]
    w = pl.program_id(0)

    @pl.when(work_ref[6, w] == 1)
    def _cast_weights():
        for p in range(n_parts):
            wb_ref[p] = w_refs[p][...].astype(BF16)

    @pl.when(work_ref[5, w] == 1)
    def _compute():
        x = x_ref[...]
        accs = [jnp.dot(x, wb_ref[p], preferred_element_type=F32) for p in range(n_parts)]
        epilogue(accs, (), o_ref, ())

    @pl.when(work_ref[5, w] == 0)
    def _zero_fill():
        o_ref[...] = jnp.zeros_like(o_ref)


def grouped_matmul(x, w, work, *, lead_base, k, part_blocks, tn, tmg, n_out, out_dtype,
                   epilogue, name):
    p = x.shape[0]
    n_parts = len(part_blocks)
    in_specs = [pl.BlockSpec((tmg, k), lambda i, wk: (wk[2, i], 0))]
    for pb in part_blocks:
        in_specs.append(pl.BlockSpec(
            (None, k, tn), lambda i, wk, pb=pb: (lead_base + wk[0, i], 0, pb + wk[1, i])))
    grid_spec = pltpu.PrefetchScalarGridSpec(
        num_scalar_prefetch=1,
        grid=(work.shape[1],),
        in_specs=in_specs,
        out_specs=pl.BlockSpec((tmg, tn), lambda i, wk: (wk[3, i], wk[4, i])),
        scratch_shapes=[pltpu.VMEM((n_parts, k, tn), BF16)])
    return pl.pallas_call(
        functools.partial(_grouped_mm_kernel, n_parts=n_parts, epilogue=epilogue),
        grid_spec=grid_spec,
        out_shape=jax.ShapeDtypeStruct((p, n_out), out_dtype),
        compiler_params=_params(1),
        name=name,
    )(work, x, *([w] * n_parts))


def _combine_kernel(pos_ref, x_ref, gate_ref, g_ref, y_hbm, *refs, tc, emit_x):
    if emit_x:
        o_ref, h_ref, buf, sems = refs
    else:
        h_ref, buf, sems = refs
    i = pl.program_id(0)
    slot = i % 2

    def issue(step, s):
        for k in range(2):
            _issue_row_gather(tc, lambda r, k=k: pos_ref[2 * (step * tc + r) + k], y_hbm,
                              buf.at[s, k], sems.at[s, k])

    @pl.when(i == 0)
    def _prime():
        issue(0, 0)

    @pl.when(i + 1 < pl.num_programs(0))
    def _prefetch():
        issue(i + 1, 1 - slot)

    for k in range(2):
        _wait_row_gather(tc, y_hbm, buf.at[slot, k], sems.at[slot, k])
    gates = gate_ref[...]
    x = x_ref[...] + gates[:, 0:1] * buf[slot, 0] + gates[:, 1:2] * buf[slot, 1]
    if emit_x:
        o_ref[...] = x
    ms = jnp.mean(x * x, axis=-1, keepdims=True)
    h_ref[...] = (x * lax.rsqrt(ms + NORM_EPS) * g_ref[...]).astype(h_ref.dtype)


def moe_combine(x, gates, y, pos, next_g, emit_x, tc=256):
    m, d = x.shape
    row_spec = pl.BlockSpec((tc, d), lambda i, pos: (i, 0))
    out_specs = [row_spec, row_spec] if emit_x else row_spec
    out_shape = ([jax.ShapeDtypeStruct((m, d), F32), jax.ShapeDtypeStruct((m, d), BF16)]
                 if emit_x else jax.ShapeDtypeStruct((m, d), F32))
    grid_spec = pltpu.PrefetchScalarGridSpec(
        num_scalar_prefetch=1,
        grid=(m // tc,),
        in_specs=[row_spec,
                  pl.BlockSpec((tc, LANES), lambda i, pos: (i, 0)),
                  pl.BlockSpec((1, d), lambda i, pos: (0, 0)),
                  pl.BlockSpec(memory_space=pl.ANY)],
        out_specs=out_specs,
        scratch_shapes=[pltpu.VMEM((2, 2, tc, d), F32), pltpu.SemaphoreType.DMA((2, 2))])
    return pl.pallas_call(
        functools.partial(_combine_kernel, tc=tc, emit_x=emit_x),
        grid_spec=grid_spec,
        out_shape=out_shape,
        compiler_params=_params(1),
        name="moe_combine",
    )(pos, x, gates, next_g.reshape(1, d), y)


def moe_swiglu(x, norm_g, w_router, w_gu, w_down, f_idx, next_g, emit_x):
    n_exp, d, two_f = w_gu.shape[1:]
    d_exp = two_f // 2
    tmg, tn_gu, tn_down = TMG, TN, min(d, 4 * TN)
    gates, idx = router(x, norm_g, w_router)
    pos, row_token, (work_gu, work_down) = moe_routing(
        idx[:, :2], n_exp, tmg, (d_exp // tn_gu, d // tn_down))
    hs = gather_norm(x, norm_g, row_token)
    act = grouped_matmul(hs, w_gu.reshape(-1, d, two_f), work_gu, lead_base=f_idx * n_exp, k=d,
                         part_blocks=(0, d_exp // tn_gu), tn=tn_gu, tmg=tmg, n_out=d_exp,
                         out_dtype=BF16, epilogue=_ep_swiglu, name="moe_gu")
    y = grouped_matmul(act, w_down.reshape(-1, d_exp, d), work_down, lead_base=f_idx * n_exp,
                       k=d_exp, part_blocks=(0,), tn=tn_down, tmg=tmg, n_out=d, out_dtype=F32,
                       epilogue=_ep_store, name="moe_down")
    return moe_combine(x, gates, y, pos, next_g, emit_x)


def kernel(x, mix_norm, ffn_norm, conv_w_in, conv_w, conv_w_out, gmlp_w_in, gmlp_v_norm_g,
           gmlp_v_norm_b, gmlp_w_s, gmlp_b_s, gmlp_w_out, sb_w_qkv, sb_w_out, dense_w_gu,
           dense_w_down, moe_w_router, moe_w_gu, moe_w_down, final_norm):
    batch, seq, d = x.shape
    depth = mix_norm.shape[0]
    x = x.reshape(batch * seq, d)
    h = None
    for i in range(depth):
        if h is None:
            h = rmsnorm(x, mix_norm[i], BF16)
        kind, j = i % 3, i // 3
        if kind == 0:
            x = short_conv_mixer(h, x, conv_w_in, conv_w, conv_w_out, j, seq)
        elif kind == 1:
            x = chunked_gmlp_mixer(h, x, gmlp_w_in, gmlp_v_norm_g, gmlp_v_norm_b, gmlp_w_s,
                                   gmlp_b_s, gmlp_w_out, j)
        else:
            x = stick_breaking_mixer(h, x, sb_w_qkv, sb_w_out, j, batch, seq)
        f = i // 2
        last = i == depth - 1
        if i % 2 == 0:
            h = rmsnorm(x, ffn_norm[i], BF16)
            x = dense_swiglu(h, x, dense_w_gu, dense_w_down, f)
            h = None
        elif last:
            out = moe_swiglu(x, ffn_norm[i], moe_w_router[f], moe_w_gu, moe_w_down, f,
                             final_norm, emit_x=False)
            return out.reshape(batch, seq, d)
        else:
            x, h = moe_swiglu(x, ffn_norm[i], moe_w_router[f], moe_w_gu, moe_w_down, f,
                              mix_norm[i + 1], emit_x=True)
    return rmsnorm(x, final_norm, F32).reshape(batch, seq, d)
```

```python
import functools

import jax
import jax.numpy as jnp
from jax import lax
from jax.experimental import pallas as pl
from jax.experimental.pallas import tpu as pltpu

F32 = jnp.float32
BF16 = jnp.bfloat16

NORM_EPS = 1e-5
CONV_WIDTH = 3
CHUNK = 128
GROUP_DIM = 128
HEAD_DIM = 128
N_EXPERTS = 8
LANES = 128
VMEM_LIMIT = 56 * 1024 * 1024
SQRT_HALF = 0.7071067811865476
LOG2_E = 1.4426950408889634

TM = 1024
TN = 512
TN_PART = 256
TQ = 256
SB_HEADS_PER_STEP = 4
TMG = 512


def _params(n_axes):
    return pltpu.CompilerParams(
        dimension_semantics=("arbitrary",) * n_axes, vmem_limit_bytes=VMEM_LIMIT)


def _rmsnorm_kernel(x_ref, g_ref, o_ref):
    x = x_ref[...]
    ms = jnp.mean(x * x, axis=-1, keepdims=True)
    o_ref[...] = (x * lax.rsqrt(ms + NORM_EPS) * g_ref[...]).astype(o_ref.dtype)


def rmsnorm(x, g, out_dtype, tr=256):
    m, d = x.shape
    return pl.pallas_call(
        _rmsnorm_kernel,
        grid=(m // tr,),
        in_specs=[pl.BlockSpec((tr, d), lambda i: (i, 0)),
                  pl.BlockSpec((1, d), lambda i: (0, 0))],
        out_specs=pl.BlockSpec((tr, d), lambda i: (i, 0)),
        out_shape=jax.ShapeDtypeStruct((m, d), out_dtype),
        compiler_params=_params(1),
        name="rmsnorm",
    )(x, g.reshape(1, d))


def _router_kernel(x_ref, g_ref, wr_ref, gate_ref, idx_ref, hp_ref):
    x = x_ref[...]
    ms = jnp.mean(x * x, axis=-1, keepdims=True)
    h = x * lax.rsqrt(ms + NORM_EPS) * g_ref[...]
    half = h.shape[1] // 2
    bits = pltpu.bitcast(h.astype(BF16).astype(F32), jnp.uint32)
    hp_ref[...] = (bits[:, :half] >> 16) | bits[:, half:]
    logits = jnp.dot(h, wr_ref[...], preferred_element_type=F32,
                     precision=lax.Precision.HIGHEST)
    lane = lax.broadcasted_iota(jnp.int32, logits.shape, 1)
    neg = jnp.float32(-jnp.inf)
    logits = jnp.where(lane < N_EXPERTS, logits, neg)
    v1 = jnp.max(logits, axis=-1, keepdims=True)
    i1 = jnp.min(jnp.where(logits == v1, lane, LANES), axis=-1, keepdims=True)
    rest = jnp.where(lane == i1, neg, logits)
    v2 = jnp.max(rest, axis=-1, keepdims=True)
    i2 = jnp.min(jnp.where(rest == v2, lane, LANES), axis=-1, keepdims=True)
    e2 = jnp.exp(v2 - v1)
    denom = 1.0 + e2
    g1 = 1.0 / denom
    g2 = e2 / denom
    gate_ref[...] = jnp.where(lane == 0, g1, jnp.where(lane == 1, g2, 0.0))
    idx_ref[...] = jnp.where(lane == 0, i1, jnp.where(lane == 1, i2, 0))


def router(x, g, w_router, tr=256):
    m, d = x.shape
    e = w_router.shape[1]
    wr = jnp.zeros((d, LANES), F32).at[:, :e].set(w_router)
    return pl.pallas_call(
        _router_kernel,
        grid=(m // tr,),
        in_specs=[pl.BlockSpec((tr, d), lambda i: (i, 0)),
                  pl.BlockSpec((1, d), lambda i: (0, 0)),
                  pl.BlockSpec((d, LANES), lambda i: (0, 0))],
        out_specs=[pl.BlockSpec((tr, LANES), lambda i: (i, 0)),
                   pl.BlockSpec((tr, LANES), lambda i: (i, 0)),
                   pl.BlockSpec((tr, d // 2), lambda i: (i, 0))],
        out_shape=[jax.ShapeDtypeStruct((m, LANES), F32),
                   jax.ShapeDtypeStruct((m, LANES), jnp.int32),
                   jax.ShapeDtypeStruct((m, d // 2), jnp.uint32)],
        compiler_params=_params(1),
        name="router",
    )(x, g.reshape(1, d), wr)


def _mm_kernel(*refs, n_parts, n_extra, n_scratch, epilogue):
    x_ref = refs[0]
    w_refs = refs[1:1 + n_parts]
    extra_refs = refs[1 + n_parts:1 + n_parts + n_extra]
    o_ref = refs[1 + n_parts + n_extra]
    wb_ref = refs[2 + n_parts + n_extra]
    scratch = refs[3 + n_parts + n_extra:]
    assert len(scratch) == n_scratch

    @pl.when(pl.program_id(1) == 0)
    def _cast_weights():
        for p in range(n_parts):
            wb_ref[p] = w_refs[p][...].astype(BF16)

    x = x_ref[...]
    accs = [jnp.dot(x, wb_ref[p], preferred_element_type=F32) for p in range(n_parts)]
    epilogue(accs, extra_refs, o_ref, scratch)


def matmul(x, w, *, w_lead, kb_x, kb_w, k, part_blocks, tn, tm, n_out, out_dtype,
           epilogue, extras=(), extra_specs=(), scratch_shapes=(), name):
    m = x.shape[0]
    n_parts = len(part_blocks)
    grid = (n_out // tn, m // tm)
    in_specs = [pl.BlockSpec((tm, k), lambda n, i: (i, kb_x))]
    for pb in part_blocks:
        in_specs.append(pl.BlockSpec((None, k, tn),
                                     lambda n, i, pb=pb: (w_lead, kb_w, pb + n)))
    in_specs.extend(extra_specs)
    kern = functools.partial(_mm_kernel, n_parts=n_parts, n_extra=len(extras),
                             n_scratch=len(scratch_shapes), epilogue=epilogue)
    return pl.pallas_call(
        kern,
        grid=grid,
        in_specs=in_specs,
        out_specs=pl.BlockSpec((tm, tn), lambda n, i: (i, n)),
        out_shape=jax.ShapeDtypeStruct((m, n_out), out_dtype),
        scratch_shapes=[pltpu.VMEM((n_parts, k, tn), BF16), *scratch_shapes],
        compiler_params=_params(2),
        name=name,
    )(x, *([w] * n_parts), *extras)


def _ep_store(accs, extra_refs, o_ref, scratch):
    o_ref[...] = accs[0].astype(o_ref.dtype)


def _ep_residual(accs, extra_refs, o_ref, scratch):
    o_ref[...] = extra_refs[0][...] + accs[0]


def _ep_swiglu(accs, extra_refs, o_ref, scratch):
    g, u = accs
    o_ref[...] = (g * jax.nn.sigmoid(g) * u).astype(o_ref.dtype)


def _ep_gelu(accs, extra_refs, o_ref, scratch):
    a = accs[0]
    o_ref[...] = (0.5 * a * (1.0 + lax.erf(a * SQRT_HALF))).astype(o_ref.dtype)


def _ep_conv(accs, extra_refs, o_ref, scratch, *, tiles_per_seq):
    b, c, xin = accs
    cw_ref, = extra_refs
    tail_ref, = scratch
    z = c * xin
    tm = z.shape[0]
    i = pl.program_id(1)

    @pl.when(i % tiles_per_seq == 0)
    def _zero_tail():
        tail_ref[...] = jnp.zeros_like(tail_ref)

    row = lax.broadcasted_iota(jnp.int32, z.shape, 0)
    t1 = tail_ref[7:8, :]
    t2 = tail_ref[6:7, :]
    z1 = jnp.where(row == 0, t1, pltpu.roll(z, 1, axis=0))
    z2 = jnp.where(row == 0, t2, jnp.where(row == 1, t1, pltpu.roll(z, 2, axis=0)))
    zc = cw_ref[0:1, :] * z2 + cw_ref[1:2, :] * z1 + cw_ref[2:3, :] * z
    o_ref[...] = (b * zc).astype(o_ref.dtype)
    tail_ref[...] = z[tm - 8:, :]


def _res_spec(tm, tn):
    return pl.BlockSpec((tm, tn), lambda n, i: (i, n))


def linear_residual(a, w, res, *, w_lead, kb_x=0, kb_w=0, k, name):
    n_out = w.shape[-1]
    tm, tn = TM, TN
    return matmul(a, w, w_lead=w_lead, kb_x=kb_x, kb_w=kb_w, k=k, part_blocks=(0,), tn=tn,
                  tm=tm, n_out=n_out, out_dtype=F32, epilogue=_ep_residual,
                  extras=(res,), extra_specs=(_res_spec(tm, tn),), name=name)


def short_conv_mixer(h, x, w_in, conv_w, w_out, j, seq):
    d = h.shape[1]
    tm, tn = min(TM, seq), TN_PART
    nb = d // tn
    cw = jnp.zeros((8, d), F32).at[:CONV_WIDTH].set(conv_w[j])
    gated = matmul(
        h, w_in, w_lead=j, kb_x=0, kb_w=0, k=d, part_blocks=(0, nb, 2 * nb), tn=tn, tm=tm,
        n_out=d, out_dtype=BF16,
        epilogue=functools.partial(_ep_conv, tiles_per_seq=seq // tm),
        extras=(cw,), extra_specs=(pl.BlockSpec((8, tn), lambda n, i: (0, n)),),
        scratch_shapes=(pltpu.VMEM((8, tn), F32),), name="conv_in")
    return linear_residual(gated, w_out, x, w_lead=j, k=d, name="conv_out")


def _gmlp_spatial_kernel(u_ref, v_ref, g_ref, b_ref, ws_ref, bst_ref, o_ref):
    v = v_ref[...]
    mu = jnp.mean(v, axis=-1, keepdims=True)
    vc = v - mu
    var = jnp.mean(vc * vc, axis=-1, keepdims=True)
    vn = (vc * lax.rsqrt(var + NORM_EPS) * g_ref[...] + b_ref[...]).astype(BF16)
    t_idx = lax.broadcasted_iota(jnp.int32, (CHUNK, CHUNK), 0)
    s_idx = lax.broadcasted_iota(jnp.int32, (CHUNK, CHUNK), 1)
    causal = s_idx <= t_idx
    n_groups = ws_ref.shape[0]
    for g in range(n_groups):
        sl = slice(g * GROUP_DIM, (g + 1) * GROUP_DIM)
        wm = jnp.where(causal, ws_ref[g], 0.0).astype(BF16)
        sv = jnp.dot(wm, vn[:, sl], preferred_element_type=F32) + bst_ref[:, g:g + 1]
        o_ref[:, sl] = (u_ref[:, sl] * sv).astype(o_ref.dtype)


def gmlp_spatial(uv, v_norm_g, v_norm_b, w_s, b_s):
    m = uv.shape[0]
    inner = uv.shape[1] // 2
    n_groups = inner // GROUP_DIM
    return pl.pallas_call(
        _gmlp_spatial_kernel,
        grid=(m // CHUNK,),
        in_specs=[pl.BlockSpec((CHUNK, inner), lambda i: (i, 0)),
                  pl.BlockSpec((CHUNK, inner), lambda i: (i, 1)),
                  pl.BlockSpec((1, inner), lambda i: (0, 0)),
                  pl.BlockSpec((1, inner), lambda i: (0, 0)),
                  pl.BlockSpec((n_groups, CHUNK, CHUNK), lambda i: (0, 0, 0)),
                  pl.BlockSpec((CHUNK, n_groups), lambda i: (0, 0))],
        out_specs=pl.BlockSpec((CHUNK, inner), lambda i: (i, 0)),
        out_shape=jax.ShapeDtypeStruct((m, inner), BF16),
        compiler_params=_params(1),
        name="gmlp_spatial",
    )(uv, uv, v_norm_g.reshape(1, inner), v_norm_b.reshape(1, inner), w_s, b_s.T)


def chunked_gmlp_mixer(h, x, w_in, v_norm_g, v_norm_b, w_s, b_s, w_out, j):
    d = h.shape[1]
    inner2 = w_in.shape[-1]
    uv = matmul(h, w_in, w_lead=j, kb_x=0, kb_w=0, k=d, part_blocks=(0,), tn=TN, tm=TM,
                n_out=inner2, out_dtype=F32, epilogue=_ep_gelu, name="gmlp_in")
    gated = gmlp_spatial(uv, v_norm_g[j], v_norm_b[j], w_s[j], b_s[j])
    return linear_residual(gated, w_out, x, w_lead=j, k=inner2 // 2, name="gmlp_out")


def _sb_attn_kernel(q_ref, k_ref, v_ref, o_ref, z_buf, hi_buf, lo_buf, acc_buf, carry_buf,
                    *, tq, scale):
    seq = q_ref.shape[0]
    n_heads = q_ref.shape[1] // HEAD_DIM
    n_blk = seq // tq
    r = lax.broadcasted_iota(jnp.int32, (tq, tq), 0)
    c = lax.broadcasted_iota(jnp.int32, (tq, tq), 1)
    strict = c < r
    suffix = (r >= c).astype(BF16)

    def head(hd):
        return slice(hd * HEAD_DIM, (hd + 1) * HEAD_DIM)

    def scores(qb, kb):
        z2 = lax.dot_general(qb, kb, (((1,), (1,)), ((), ())), preferred_element_type=F32)
        z2 = z2 * (scale * LOG2_E)
        sp = jnp.maximum(z2, 0.0) + jnp.log2(1.0 + jnp.exp2(-jnp.abs(z2)))
        return z2, sp

    def split(sp):
        hi = sp.astype(BF16)
        return hi, (sp - hi.astype(F32)).astype(BF16)

    def suffix_sum(hi, lo):
        return (jnp.dot(hi, suffix, preferred_element_type=F32)
                + jnp.dot(lo, suffix, preferred_element_type=F32))

    heads = range(n_heads)

    def score_head(hd, slot, q0, k0, diagonal):
        z, sp = scores(q_ref[pl.ds(q0, tq), head(hd)], k_ref[pl.ds(k0, tq), head(hd)])
        hi, lo = split(jnp.where(strict, sp, 0.0) if diagonal else sp)
        z_buf[slot, hd] = z
        hi_buf[slot, hd] = hi
        lo_buf[slot, hd] = lo

    def weights_head(hd, slot, diagonal):
        s = suffix_sum(hi_buf[slot, hd], lo_buf[slot, hd])
        if diagonal:
            att = jnp.where(strict, jnp.exp2(z_buf[slot, hd] - s), 0.0)
            carry_buf[hd] = s[:, 0:1]
        else:
            att = jnp.exp2(z_buf[slot, hd] - s - carry_buf[hd])
            carry_buf[hd] += s[:, 0:1]
        return att.astype(BF16)

    def values_head(hd, att, k0, diagonal):
        pv = jnp.dot(att, v_ref[pl.ds(k0, tq), head(hd)], preferred_element_type=F32)
        if diagonal:
            acc_buf[hd] = pv
        else:
            acc_buf[hd] += pv

    def score_stage(slot, q0, k0, diagonal):
        for hd in heads:
            score_head(hd, slot, q0, k0, diagonal)

    def value_stage(slot, k0, diagonal):
        atts = [weights_head(hd, slot, diagonal) for hd in heads]
        for hd in heads:
            values_head(hd, atts[hd], k0, diagonal)

    def write_out(q0):
        for hd in heads:
            o_ref[pl.ds(q0, tq), head(hd)] = acc_buf[hd].astype(o_ref.dtype)

    score_stage(0, 0, 0, True)
    value_stage(0, 0, True)
    write_out(0)

    def q_block(qi, _):
        q0 = pl.multiple_of(qi * tq, tq)
        score_stage(1, q0, q0, True)
        score_stage(0, q0, pl.multiple_of((qi - 1) * tq, tq), False)
        value_stage(1, q0, True)

        def kv_block(t, _):
            k_cur = pl.multiple_of((qi - t) * tq, tq)
            k_next = pl.multiple_of((qi - t - 1) * tq, tq)
            for cur in range(2):
                @pl.when((t - 1) % 2 == cur)
                def _step():
                    for hd in heads:
                        score_head(hd, 1 - cur, q0, k_next, False)
                        values_head(hd, weights_head(hd, cur, False), k_cur, False)
            return 0

        lax.fori_loop(1, qi, kv_block, 0)
        for cur in range(2):
            @pl.when((qi - 1) % 2 == cur)
            def _last():
                value_stage(cur, 0, False)
        write_out(q0)
        return 0

    lax.fori_loop(1, n_blk, q_block, 0)


def sb_attention(qkv, batch, seq):
    m, three_d = qkv.shape
    n_heads = three_d // (3 * HEAD_DIM)
    hb = min(SB_HEADS_PER_STEP, n_heads)
    n_groups = n_heads // hb
    width = hb * HEAD_DIM
    kern = functools.partial(_sb_attn_kernel, tq=TQ, scale=HEAD_DIM ** -0.5)
    return pl.pallas_call(
        kern,
        grid=(batch, n_groups),
        in_specs=[pl.BlockSpec((seq, width), lambda b, h: (b, h)),
                  pl.BlockSpec((seq, width), lambda b, h: (b, n_groups + h)),
                  pl.BlockSpec((seq, width), lambda b, h: (b, 2 * n_groups + h))],
        out_specs=pl.BlockSpec((seq, width), lambda b, h: (b, h)),
        out_shape=jax.ShapeDtypeStruct((m, n_heads * HEAD_DIM), BF16),
        scratch_shapes=[pltpu.VMEM((2, hb, TQ, TQ), F32), pltpu.VMEM((2, hb, TQ, TQ), BF16),
                        pltpu.VMEM((2, hb, TQ, TQ), BF16), pltpu.VMEM((hb, TQ, HEAD_DIM), F32),
                        pltpu.VMEM((hb, TQ, 1), F32)],
        compiler_params=_params(2),
        name="sb_attention",
    )(qkv, qkv, qkv)


def stick_breaking_mixer(h, x, w_qkv, w_out, j, batch, seq):
    d = h.shape[1]
    qkv = matmul(h, w_qkv, w_lead=j, kb_x=0, kb_w=0, k=d, part_blocks=(0,), tn=TN, tm=TM,
                 n_out=w_qkv.shape[-1], out_dtype=BF16, epilogue=_ep_store, name="sb_qkv")
    o = sb_attention(qkv, batch, seq)
    return linear_residual(o, w_out, x, w_lead=j, k=w_out.shape[1], name="sb_out")


def swiglu_hidden(h, w_gu, w_lead, name):
    d = h.shape[1]
    f = w_gu.shape[-1] // 2
    tm, tn = TM, TN_PART
    return matmul(h, w_gu, w_lead=w_lead, kb_x=0, kb_w=0, k=d, part_blocks=(0, f // tn),
                  tn=tn, tm=tm, n_out=f, out_dtype=BF16, epilogue=_ep_swiglu, name=name)


def dense_swiglu(h, x, w_gu, w_down, f_idx):
    k_split = w_gu.shape[1]
    a = swiglu_hidden(h, w_gu, f_idx, "dense_gu")
    for kb in range(w_down.shape[1] // k_split):
        x = linear_residual(a, w_down, x, w_lead=f_idx, kb_x=kb, kb_w=kb, k=k_split,
                            name="dense_down")
    return x


def moe_routing(idx2, n_exp, tmg, n_col_tiles):
    m = idx2.shape[0]
    n_slots = 2 * m
    t_bound = n_slots // tmg + n_exp
    flat = idx2.reshape(-1)
    onehot = (flat[:, None] == jnp.arange(n_exp, dtype=jnp.int32)[None, :]).astype(jnp.int32)
    rank = jnp.sum((jnp.cumsum(onehot, axis=0) - onehot) * onehot, axis=1)
    counts = jnp.sum(onehot, axis=0)
    tiles = (counts + tmg - 1) // tmg
    tile_end = jnp.cumsum(tiles)
    tile_start = tile_end - tiles
    used = tile_end[-1]
    pos = tile_start[flat] * tmg + rank
    row_token = jnp.zeros((t_bound * tmg,), jnp.int32).at[pos].set(
        jnp.arange(n_slots, dtype=jnp.int32) // 2)

    def work_list(nt):
        item_end = jnp.cumsum(tiles * nt)
        total = item_end[-1]
        w = jnp.arange(nt * t_bound, dtype=jnp.int32)
        valid = w < total
        wc = jnp.minimum(w, total - 1)
        e = jnp.sum((item_end[None, :] <= wc[:, None]).astype(jnp.int32), axis=1)
        local = wc - (item_end - tiles * nt)[e]
        r = jnp.maximum(tiles[e], 1)
        n = local // r
        row = tile_start[e] + local % r
        first = valid & (local % r == 0)
        k = w - total
        out_row = jnp.where(valid, row, used + k // nt)
        out_col = jnp.where(valid, n, k % nt)
        return jnp.stack([e, n, row, out_row, out_col, valid.astype(jnp.int32),
                          first.astype(jnp.int32)]).astype(jnp.int32)

    return pos, row_token, [work_list(nt) for nt in n_col_tiles]


GATHER_UNROLL = 8
COMBINE_ROWS = 16


def _issue_row_gather(n_rows, src_index, src_hbm, dst_ref, sem):
    def body(group, _):
        for u in range(GATHER_UNROLL):
            r = GATHER_UNROLL * group + u
            pltpu.make_async_copy(src_hbm.at[pl.ds(src_index(r), 1)],
                                  dst_ref.at[pl.ds(r, 1)], sem).start(priority=u % 2)
        return 0
    lax.fori_loop(0, n_rows // GATHER_UNROLL, body, 0)


def _wait_row_gather(n_rows, src_hbm, dst_ref, sem):
    pltpu.make_async_copy(src_hbm.at[pl.ds(0, n_rows)], dst_ref, sem).wait()


def _gather_unpack_kernel(tok_ref, hp_hbm, o_ref, buf, sems, *, tg):
    i = pl.program_id(0)
    slot = i % 2

    def issue(step, s):
        _issue_row_gather(tg, lambda r: tok_ref[step * tg + r], hp_hbm, buf.at[s], sems.at[s])

    @pl.when(i == 0)
    def _prime():
        issue(0, 0)

    @pl.when(i + 1 < pl.num_programs(0))
    def _prefetch():
        issue(i + 1, 1 - slot)

    _wait_row_gather(tg, hp_hbm, buf.at[slot], sems.at[slot])
    packed = buf[slot]
    half = packed.shape[1]
    o_ref[:, :half] = pltpu.bitcast(packed << 16, F32).astype(o_ref.dtype)
    o_ref[:, half:] = pltpu.bitcast(packed & jnp.uint32(0xFFFF0000), F32).astype(o_ref.dtype)


def gather_rows(hp, row_token, tg=256):
    half = hp.shape[1]
    p = row_token.shape[0]
    grid_spec = pltpu.PrefetchScalarGridSpec(
        num_scalar_prefetch=1,
        grid=(p // tg,),
        in_specs=[pl.BlockSpec(memory_space=pl.ANY)],
        out_specs=pl.BlockSpec((tg, 2 * half), lambda i, tok: (i, 0)),
        scratch_shapes=[pltpu.VMEM((2, tg, half), jnp.uint32), pltpu.SemaphoreType.DMA((2,))])
    return pl.pallas_call(
        functools.partial(_gather_unpack_kernel, tg=tg),
        grid_spec=grid_spec,
        out_shape=jax.ShapeDtypeStruct((p, 2 * half), BF16),
        compiler_params=_params(1),
        name="moe_gather",
    )(row_token, hp)


def _grouped_mm_kernel(work_ref, x_ref, *refs, n_parts, epilogue):
    w_refs = refs[:n_parts]
    o_ref = refs[n_parts]
    wb_ref = refs[n_parts + 1]
    w = pl.program_id(0)

    @pl.when(work_ref[6, w] == 1)
    def _cast_weights():
        for p in range(n_parts):
            wb_ref[p] = w_refs[p][...].astype(BF16)

    @pl.when(work_ref[5, w] == 1)
    def _compute():
        x = x_ref[...]
        accs = [jnp.dot(x, wb_ref[p], preferred_element_type=F32) for p in range(n_parts)]
        epilogue(accs, (), o_ref, ())

    @pl.when(work_ref[5, w] == 0)
    def _zero_fill():
        o_ref[...] = jnp.zeros_like(o_ref)


def grouped_matmul(x, w, work, *, lead_base, k, part_blocks, tn, tmg, n_out, out_dtype,
                   epilogue, name):
    p = x.shape[0]
    n_parts = len(part_blocks)
    in_specs = [pl.BlockSpec((tmg, k), lambda i, wk: (wk[2, i], 0))]
    for pb in part_blocks:
        in_specs.append(pl.BlockSpec(
            (None, k, tn), lambda i, wk, pb=pb: (lead_base + wk[0, i], 0, pb + wk[1, i])))
    grid_spec = pltpu.PrefetchScalarGridSpec(
        num_scalar_prefetch=1,
        grid=(work.shape[1],),
        in_specs=in_specs,
        out_specs=pl.BlockSpec((tmg, tn), lambda i, wk: (wk[3, i], wk[4, i])),
        scratch_shapes=[pltpu.VMEM((n_parts, k, tn), BF16)])
    return pl.pallas_call(
        functools.partial(_grouped_mm_kernel, n_parts=n_parts, epilogue=epilogue),
        grid_spec=grid_spec,
        out_shape=jax.ShapeDtypeStruct((p, n_out), out_dtype),
        compiler_params=_params(1),
        name=name,
    )(work, x, *([w] * n_parts))


def _combine_kernel(pos_ref, x_ref, gate_ref, g_ref, y_hbm, *refs, tc, emit_x):
    if emit_x:
        o_ref, h_ref, buf, sems = refs
    else:
        h_ref, buf, sems = refs
    i = pl.program_id(0)
    slot = i % 2

    def issue(step, s):
        for k in range(2):
            _issue_row_gather(tc, lambda r, k=k: pos_ref[2 * (step * tc + r) + k], y_hbm,
                              buf.at[s, k], sems.at[s, k])

    @pl.when(i == 0)
    def _prime():
        issue(0, 0)

    @pl.when(i + 1 < pl.num_programs(0))
    def _prefetch():
        issue(i + 1, 1 - slot)

    for k in range(2):
        _wait_row_gather(tc, y_hbm, buf.at[slot, k], sems.at[slot, k])
    def chunk(c, _):
        rows = pl.ds(pl.multiple_of(c * COMBINE_ROWS, COMBINE_ROWS), COMBINE_ROWS)
        gates = gate_ref[rows, :]
        x = (x_ref[rows, :] + gates[:, 0:1] * buf[slot, 0, rows, :]
             + gates[:, 1:2] * buf[slot, 1, rows, :])
        if emit_x:
            o_ref[rows, :] = x
        ms = jnp.mean(x * x, axis=-1, keepdims=True)
        h_ref[rows, :] = (x * lax.rsqrt(ms + NORM_EPS) * g_ref[...]).astype(h_ref.dtype)
        return 0

    lax.fori_loop(0, tc // COMBINE_ROWS, chunk, 0, unroll=4)


def moe_combine(x, gates, y, pos, next_g, emit_x, tc=256):
    m, d = x.shape
    row_spec = pl.BlockSpec((tc, d), lambda i, pos: (i, 0))
    out_specs = [row_spec, row_spec] if emit_x else row_spec
    out_shape = ([jax.ShapeDtypeStruct((m, d), F32), jax.ShapeDtypeStruct((m, d), BF16)]
                 if emit_x else jax.ShapeDtypeStruct((m, d), F32))
    grid_spec = pltpu.PrefetchScalarGridSpec(
        num_scalar_prefetch=1,
        grid=(m // tc,),
        in_specs=[row_spec,
                  pl.BlockSpec((tc, LANES), lambda i, pos: (i, 0)),
                  pl.BlockSpec((1, d), lambda i, pos: (0, 0)),
                  pl.BlockSpec(memory_space=pl.ANY)],
        out_specs=out_specs,
        scratch_shapes=[pltpu.VMEM((2, 2, tc, d), F32), pltpu.SemaphoreType.DMA((2, 2))])
    return pl.pallas_call(
        functools.partial(_combine_kernel, tc=tc, emit_x=emit_x),
        grid_spec=grid_spec,
        out_shape=out_shape,
        compiler_params=_params(1),
        name="moe_combine",
    )(pos, x, gates, next_g.reshape(1, d), y)


def moe_swiglu(x, norm_g, w_router, w_gu, w_down, f_idx, next_g, emit_x):
    n_exp, d, two_f = w_gu.shape[1:]
    d_exp = two_f // 2
    tmg, tn_gu, tn_down = TMG, TN, min(d, 4 * TN)
    gates, idx, hp = router(x, norm_g, w_router)
    pos, row_token, (work_gu, work_down) = moe_routing(
        idx[:, :2], n_exp, tmg, (d_exp // tn_gu, d // tn_down))
    hs = gather_rows(hp, row_token)
    act = grouped_matmul(hs, w_gu.reshape(-1, d, two_f), work_gu, lead_base=f_idx * n_exp, k=d,
                         part_blocks=(0, d_exp // tn_gu), tn=tn_gu, tmg=tmg, n_out=d_exp,
                         out_dtype=BF16, epilogue=_ep_swiglu, name="moe_gu")
    y = grouped_matmul(act, w_down.reshape(-1, d_exp, d), work_down, lead_base=f_idx * n_exp,
                       k=d_exp, part_blocks=(0,), tn=tn_down, tmg=tmg, n_out=d, out_dtype=F32,
                       epilogue=_ep_store, name="moe_down")
    return moe_combine(x, gates, y, pos, next_g, emit_x)


def kernel(x, mix_norm, ffn_norm, conv_w_in, conv_w, conv_w_out, gmlp_w_in, gmlp_v_norm_g,
           gmlp_v_norm_b, gmlp_w_s, gmlp_b_s, gmlp_w_out, sb_w_qkv, sb_w_out, dense_w_gu,
           dense_w_down, moe_w_router, moe_w_gu, moe_w_down, final_norm):
    batch, seq, d = x.shape
    depth = mix_norm.shape[0]
    x = x.reshape(batch * seq, d)
    h = None
    for i in range(depth):
        if h is None:
            h = rmsnorm(x, mix_norm[i], BF16)
        kind, j = i % 3, i // 3
        if kind == 0:
            x = short_conv_mixer(h, x, conv_w_in, conv_w, conv_w_out, j, seq)
        elif kind == 1:
            x = chunked_gmlp_mixer(h, x, gmlp_w_in, gmlp_v_norm_g, gmlp_v_norm_b, gmlp_w_s,
                                   gmlp_b_s, gmlp_w_out, j)
        else:
            x = stick_breaking_mixer(h, x, sb_w_qkv, sb_w_out, j, batch, seq)
        f = i // 2
        last = i == depth - 1
        if i % 2 == 0:
            h = rmsnorm(x, ffn_norm[i], BF16)
            x = dense_swiglu(h, x, dense_w_gu, dense_w_down, f)
            h = None
        elif last:
            out = moe_swiglu(x, ffn_norm[i], moe_w_router[f], moe_w_gu, moe_w_down, f,
                             final_norm, emit_x=False)
            return out.reshape(batch, seq, d)
        else:
            x, h = moe_swiglu(x, ffn_norm[i], moe_w_router[f], moe_w_gu, moe_w_down, f,
                              mix_norm[i + 1], emit_x=True)
    return rmsnorm(x, final_norm, F32).reshape(batch, seq, d)
```

```python
import functools

import jax
import jax.numpy as jnp
from jax import lax
from jax.experimental import pallas as pl
from jax.experimental.pallas import tpu as pltpu

F32 = jnp.float32
BF16 = jnp.bfloat16

NORM_EPS = 1e-5
CONV_WIDTH = 3
CHUNK = 128
GROUP_DIM = 128
HEAD_DIM = 128
N_EXPERTS = 8
LANES = 128
VMEM_LIMIT = 61 * 1024 * 1024
SQRT_HALF = 0.7071067811865476
LOG2_E = 1.4426950408889634

TM = 1024
TN_PART = 256
TM_WIDE = 512
TN_WIDE = 1024
TN = 512
TQ = 256
SB_HEADS_PER_STEP = 4
TMG = 512


def _params(n_axes):
    return pltpu.CompilerParams(
        dimension_semantics=("arbitrary",) * n_axes, vmem_limit_bytes=VMEM_LIMIT)


def _rmsnorm_kernel(x_ref, g_ref, o_ref):
    x = x_ref[...]
    ms = jnp.mean(x * x, axis=-1, keepdims=True)
    o_ref[...] = (x * lax.rsqrt(ms + NORM_EPS) * g_ref[...]).astype(o_ref.dtype)


def rmsnorm(x, g, out_dtype, tr=256):
    m, d = x.shape
    return pl.pallas_call(
        _rmsnorm_kernel,
        grid=(m // tr,),
        in_specs=[pl.BlockSpec((tr, d), lambda i: (i, 0)),
                  pl.BlockSpec((1, d), lambda i: (0, 0))],
        out_specs=pl.BlockSpec((tr, d), lambda i: (i, 0)),
        out_shape=jax.ShapeDtypeStruct((m, d), out_dtype),
        compiler_params=_params(1),
        name="rmsnorm",
    )(x, g.reshape(1, d))


def _router_kernel(x_ref, g_ref, wr_ref, gate_ref, idx_ref, hp_ref):
    x = x_ref[...]
    ms = jnp.mean(x * x, axis=-1, keepdims=True)
    h = x * lax.rsqrt(ms + NORM_EPS) * g_ref[...]
    half = h.shape[1] // 2
    bits = pltpu.bitcast(h.astype(BF16).astype(F32), jnp.uint32)
    hp_ref[...] = (bits[:, :half] >> 16) | bits[:, half:]
    h_hi = h.astype(BF16)
    h_lo = (h - h_hi.astype(F32)).astype(BF16)
    logits = (jnp.dot(h_hi, wr_ref[0], preferred_element_type=F32)
              + jnp.dot(h_lo, wr_ref[0], preferred_element_type=F32)
              + jnp.dot(h_hi, wr_ref[1], preferred_element_type=F32))
    lane = lax.broadcasted_iota(jnp.int32, logits.shape, 1)
    neg = jnp.float32(-jnp.inf)
    logits = jnp.where(lane < N_EXPERTS, logits, neg)
    v1 = jnp.max(logits, axis=-1, keepdims=True)
    i1 = jnp.min(jnp.where(logits == v1, lane, LANES), axis=-1, keepdims=True)
    rest = jnp.where(lane == i1, neg, logits)
    v2 = jnp.max(rest, axis=-1, keepdims=True)
    i2 = jnp.min(jnp.where(rest == v2, lane, LANES), axis=-1, keepdims=True)
    e2 = jnp.exp(v2 - v1)
    denom = 1.0 + e2
    g1 = 1.0 / denom
    g2 = e2 / denom
    gate_ref[...] = jnp.where(lane == 0, g1, jnp.where(lane == 1, g2, 0.0))
    idx_ref[...] = jnp.where(lane == 0, i1, jnp.where(lane == 1, i2, 0))


def router(x, g, w_router, tr=256):
    m, d = x.shape
    e = w_router.shape[1]
    wr = jnp.zeros((d, LANES), F32).at[:, :e].set(w_router)
    wr_hi = wr.astype(BF16)
    wr = jnp.stack([wr_hi, (wr - wr_hi.astype(F32)).astype(BF16)])
    return pl.pallas_call(
        _router_kernel,
        grid=(m // tr,),
        in_specs=[pl.BlockSpec((tr, d), lambda i: (i, 0)),
                  pl.BlockSpec((1, d), lambda i: (0, 0)),
                  pl.BlockSpec((2, d, LANES), lambda i: (0, 0, 0))],
        out_specs=[pl.BlockSpec((tr, LANES), lambda i: (i, 0)),
                   pl.BlockSpec((tr, LANES), lambda i: (i, 0)),
                   pl.BlockSpec((tr, d // 2), lambda i: (i, 0))],
        out_shape=[jax.ShapeDtypeStruct((m, LANES), F32),
                   jax.ShapeDtypeStruct((m, LANES), jnp.int32),
                   jax.ShapeDtypeStruct((m, d // 2), jnp.uint32)],
        compiler_params=_params(1),
        name="router",
    )(x, g.reshape(1, d), wr)


def _mm_kernel(*refs, n_parts, n_extra, n_scratch, epilogue):
    x_ref = refs[0]
    w_refs = refs[1:1 + n_parts]
    extra_refs = refs[1 + n_parts:1 + n_parts + n_extra]
    o_ref = refs[1 + n_parts + n_extra]
    wb_ref = refs[2 + n_parts + n_extra]
    scratch = refs[3 + n_parts + n_extra:]
    assert len(scratch) == n_scratch

    @pl.when(pl.program_id(1) == 0)
    def _cast_weights():
        for p in range(n_parts):
            wb_ref[p] = w_refs[p][...].astype(BF16)

    x = x_ref[...]
    accs = [jnp.dot(x, wb_ref[p], preferred_element_type=F32) for p in range(n_parts)]
    epilogue(accs, extra_refs, o_ref, scratch)


def matmul(x, w, *, w_lead, kb_x, kb_w, k, part_blocks, tn, tm, n_out, out_dtype,
           epilogue, extras=(), extra_specs=(), scratch_shapes=(), name):
    m = x.shape[0]
    n_parts = len(part_blocks)
    grid = (n_out // tn, m // tm)
    in_specs = [pl.BlockSpec((tm, k), lambda n, i: (i, kb_x))]
    for pb in part_blocks:
        in_specs.append(pl.BlockSpec((None, k, tn),
                                     lambda n, i, pb=pb: (w_lead, kb_w, pb + n)))
    in_specs.extend(extra_specs)
    kern = functools.partial(_mm_kernel, n_parts=n_parts, n_extra=len(extras),
                             n_scratch=len(scratch_shapes), epilogue=epilogue)
    return pl.pallas_call(
        kern,
        grid=grid,
        in_specs=in_specs,
        out_specs=pl.BlockSpec((tm, tn), lambda n, i: (i, n)),
        out_shape=jax.ShapeDtypeStruct((m, n_out), out_dtype),
        scratch_shapes=[pltpu.VMEM((n_parts, k, tn), BF16), *scratch_shapes],
        compiler_params=_params(2),
        name=name,
    )(x, *([w] * n_parts), *extras)


def _ep_store(accs, extra_refs, o_ref, scratch):
    o_ref[...] = accs[0].astype(o_ref.dtype)


def _ep_residual(accs, extra_refs, o_ref, scratch):
    o_ref[...] = extra_refs[0][...] + accs[0]


def _ep_swiglu(accs, extra_refs, o_ref, scratch):
    g, u = accs
    o_ref[...] = (g * jax.nn.sigmoid(g) * u).astype(o_ref.dtype)


def _ep_gelu(accs, extra_refs, o_ref, scratch):
    a = accs[0]
    o_ref[...] = (0.5 * a * (1.0 + lax.erf(a * SQRT_HALF))).astype(o_ref.dtype)


def _ep_conv(accs, extra_refs, o_ref, scratch, *, tiles_per_seq):
    b, c, xin = accs
    cw_ref, = extra_refs
    tail_ref, = scratch
    z = c * xin
    tm = z.shape[0]
    i = pl.program_id(1)

    @pl.when(i % tiles_per_seq == 0)
    def _zero_tail():
        tail_ref[...] = jnp.zeros_like(tail_ref)

    row = lax.broadcasted_iota(jnp.int32, z.shape, 0)
    t1 = tail_ref[7:8, :]
    t2 = tail_ref[6:7, :]
    z1 = jnp.where(row == 0, t1, pltpu.roll(z, 1, axis=0))
    z2 = jnp.where(row == 0, t2, jnp.where(row == 1, t1, pltpu.roll(z, 2, axis=0)))
    zc = cw_ref[0:1, :] * z2 + cw_ref[1:2, :] * z1 + cw_ref[2:3, :] * z
    o_ref[...] = (b * zc).astype(o_ref.dtype)
    tail_ref[...] = z[tm - 8:, :]


def _res_spec(tm, tn):
    return pl.BlockSpec((tm, tn), lambda n, i: (i, n))


def linear_residual(a, w, res, *, w_lead, kb_x=0, kb_w=0, k, name):
    n_out = w.shape[-1]
    tm, tn = TM_WIDE, TN_WIDE
    return matmul(a, w, w_lead=w_lead, kb_x=kb_x, kb_w=kb_w, k=k, part_blocks=(0,), tn=tn,
                  tm=tm, n_out=n_out, out_dtype=F32, epilogue=_ep_residual,
                  extras=(res,), extra_specs=(_res_spec(tm, tn),), name=name)


def short_conv_mixer(h, x, w_in, conv_w, w_out, j, seq):
    d = h.shape[1]
    tm, tn = min(TM, seq), TN_PART
    nb = d // tn
    cw = jnp.zeros((8, d), F32).at[:CONV_WIDTH].set(conv_w[j])
    gated = matmul(
        h, w_in, w_lead=j, kb_x=0, kb_w=0, k=d, part_blocks=(0, nb, 2 * nb), tn=tn, tm=tm,
        n_out=d, out_dtype=BF16,
        epilogue=functools.partial(_ep_conv, tiles_per_seq=seq // tm),
        extras=(cw,), extra_specs=(pl.BlockSpec((8, tn), lambda n, i: (0, n)),),
        scratch_shapes=(pltpu.VMEM((8, tn), F32),), name="conv_in")
    return linear_residual(gated, w_out, x, w_lead=j, k=d, name="conv_out")


def _gmlp_spatial_kernel(u_ref, v_ref, g_ref, b_ref, ws_ref, bst_ref, o_ref):
    v = v_ref[...]
    mu = jnp.mean(v, axis=-1, keepdims=True)
    vc = v - mu
    var = jnp.mean(vc * vc, axis=-1, keepdims=True)
    vn = (vc * lax.rsqrt(var + NORM_EPS) * g_ref[...] + b_ref[...]).astype(BF16)
    t_idx = lax.broadcasted_iota(jnp.int32, (CHUNK, CHUNK), 0)
    s_idx = lax.broadcasted_iota(jnp.int32, (CHUNK, CHUNK), 1)
    causal = s_idx <= t_idx
    n_groups = ws_ref.shape[0]
    for g in range(n_groups):
        sl = slice(g * GROUP_DIM, (g + 1) * GROUP_DIM)
        wm = jnp.where(causal, ws_ref[g], 0.0).astype(BF16)
        sv = jnp.dot(wm, vn[:, sl], preferred_element_type=F32) + bst_ref[:, g:g + 1]
        o_ref[:, sl] = (u_ref[:, sl] * sv).astype(o_ref.dtype)


def gmlp_spatial(uv, v_norm_g, v_norm_b, w_s, b_s):
    m = uv.shape[0]
    inner = uv.shape[1] // 2
    n_groups = inner // GROUP_DIM
    return pl.pallas_call(
        _gmlp_spatial_kernel,
        grid=(m // CHUNK,),
        in_specs=[pl.BlockSpec((CHUNK, inner), lambda i: (i, 0)),
                  pl.BlockSpec((CHUNK, inner), lambda i: (i, 1)),
                  pl.BlockSpec((1, inner), lambda i: (0, 0)),
                  pl.BlockSpec((1, inner), lambda i: (0, 0)),
                  pl.BlockSpec((n_groups, CHUNK, CHUNK), lambda i: (0, 0, 0)),
                  pl.BlockSpec((CHUNK, n_groups), lambda i: (0, 0))],
        out_specs=pl.BlockSpec((CHUNK, inner), lambda i: (i, 0)),
        out_shape=jax.ShapeDtypeStruct((m, inner), BF16),
        compiler_params=_params(1),
        name="gmlp_spatial",
    )(uv, uv, v_norm_g.reshape(1, inner), v_norm_b.reshape(1, inner), w_s, b_s.T)


def chunked_gmlp_mixer(h, x, w_in, v_norm_g, v_norm_b, w_s, b_s, w_out, j):
    d = h.shape[1]
    inner2 = w_in.shape[-1]
    uv = matmul(h, w_in, w_lead=j, kb_x=0, kb_w=0, k=d, part_blocks=(0,), tn=TN_WIDE,
                tm=TM_WIDE, n_out=inner2, out_dtype=F32, epilogue=_ep_gelu, name="gmlp_in")
    gated = gmlp_spatial(uv, v_norm_g[j], v_norm_b[j], w_s[j], b_s[j])
    return linear_residual(gated, w_out, x, w_lead=j, k=inner2 // 2, name="gmlp_out")


def _sb_attn_kernel(q_ref, k_ref, v_ref, o_ref, z_buf, hi_buf, lo_buf, acc_buf, carry_buf,
                    *, tq, scale):
    seq = q_ref.shape[0]
    n_heads = q_ref.shape[1] // HEAD_DIM
    n_blk = seq // tq
    r = lax.broadcasted_iota(jnp.int32, (tq, tq), 0)
    c = lax.broadcasted_iota(jnp.int32, (tq, tq), 1)
    strict = c < r
    suffix = (r >= c).astype(BF16)

    def head(hd):
        return slice(hd * HEAD_DIM, (hd + 1) * HEAD_DIM)

    def scores(qb, kb):
        z2 = lax.dot_general(qb, kb, (((1,), (1,)), ((), ())), preferred_element_type=F32)
        z2 = z2 * (scale * LOG2_E)
        sp = jnp.maximum(z2, 0.0) + jnp.log2(1.0 + jnp.exp2(-jnp.abs(z2)))
        return z2, sp

    def split(sp):
        hi = sp.astype(BF16)
        return hi, (sp - hi.astype(F32)).astype(BF16)

    def suffix_sum(hi, lo):
        return (jnp.dot(hi, suffix, preferred_element_type=F32)
                + jnp.dot(lo, suffix, preferred_element_type=F32))

    heads = range(n_heads)

    def score_head(hd, slot, q0, k0, diagonal):
        z, sp = scores(q_ref[pl.ds(q0, tq), head(hd)], k_ref[pl.ds(k0, tq), head(hd)])
        hi, lo = split(jnp.where(strict, sp, 0.0) if diagonal else sp)
        z_buf[slot, hd] = z
        hi_buf[slot, hd] = hi
        lo_buf[slot, hd] = lo

    def weights_head(hd, slot, diagonal):
        s = suffix_sum(hi_buf[slot, hd], lo_buf[slot, hd])
        if diagonal:
            att = jnp.where(strict, jnp.exp2(z_buf[slot, hd] - s), 0.0)
            carry_buf[hd] = s[:, 0:1]
        else:
            att = jnp.exp2(z_buf[slot, hd] - s - carry_buf[hd])
            carry_buf[hd] += s[:, 0:1]
        return att.astype(BF16)

    def values_head(hd, att, k0, diagonal):
        pv = jnp.dot(att, v_ref[pl.ds(k0, tq), head(hd)], preferred_element_type=F32)
        if diagonal:
            acc_buf[hd] = pv
        else:
            acc_buf[hd] += pv

    def score_stage(slot, q0, k0, diagonal):
        for hd in heads:
            score_head(hd, slot, q0, k0, diagonal)

    def value_stage(slot, k0, diagonal):
        atts = [weights_head(hd, slot, diagonal) for hd in heads]
        for hd in heads:
            values_head(hd, atts[hd], k0, diagonal)

    def write_out(q0):
        for hd in heads:
            o_ref[pl.ds(q0, tq), head(hd)] = acc_buf[hd].astype(o_ref.dtype)

    score_stage(0, 0, 0, True)
    value_stage(0, 0, True)
    write_out(0)

    def q_block(qi, _):
        q0 = pl.multiple_of(qi * tq, tq)
        score_stage(1, q0, q0, True)
        score_stage(0, q0, pl.multiple_of((qi - 1) * tq, tq), False)
        value_stage(1, q0, True)

        def kv_block(t, _):
            k_cur = pl.multiple_of((qi - t) * tq, tq)
            k_next = pl.multiple_of((qi - t - 1) * tq, tq)
            for cur in range(2):
                @pl.when((t - 1) % 2 == cur)
                def _step():
                    for hd in heads:
                        score_head(hd, 1 - cur, q0, k_next, False)
                        values_head(hd, weights_head(hd, cur, False), k_cur, False)
            return 0

        lax.fori_loop(1, qi, kv_block, 0)
        for cur in range(2):
            @pl.when((qi - 1) % 2 == cur)
            def _last():
                value_stage(cur, 0, False)
        write_out(q0)
        return 0

    lax.fori_loop(1, n_blk, q_block, 0)


def sb_attention(qkv, batch, seq):
    m, three_d = qkv.shape
    n_heads = three_d // (3 * HEAD_DIM)
    hb = min(SB_HEADS_PER_STEP, n_heads)
    n_groups = n_heads // hb
    width = hb * HEAD_DIM
    kern = functools.partial(_sb_attn_kernel, tq=TQ, scale=HEAD_DIM ** -0.5)
    return pl.pallas_call(
        kern,
        grid=(batch, n_groups),
        in_specs=[pl.BlockSpec((seq, width), lambda b, h: (b, h)),
                  pl.BlockSpec((seq, width), lambda b, h: (b, n_groups + h)),
                  pl.BlockSpec((seq, width), lambda b, h: (b, 2 * n_groups + h))],
        out_specs=pl.BlockSpec((seq, width), lambda b, h: (b, h)),
        out_shape=jax.ShapeDtypeStruct((m, n_heads * HEAD_DIM), BF16),
        scratch_shapes=[pltpu.VMEM((2, hb, TQ, TQ), F32), pltpu.VMEM((2, hb, TQ, TQ), BF16),
                        pltpu.VMEM((2, hb, TQ, TQ), BF16), pltpu.VMEM((hb, TQ, HEAD_DIM), F32),
                        pltpu.VMEM((hb, TQ, 1), F32)],
        compiler_params=_params(2),
        name="sb_attention",
    )(qkv, qkv, qkv)


def stick_breaking_mixer(h, x, w_qkv, w_out, j, batch, seq):
    d = h.shape[1]
    qkv = matmul(h, w_qkv, w_lead=j, kb_x=0, kb_w=0, k=d, part_blocks=(0,), tn=TN_WIDE,
                 tm=TM_WIDE, n_out=w_qkv.shape[-1], out_dtype=BF16, epilogue=_ep_store, name="sb_qkv")
    o = sb_attention(qkv, batch, seq)
    return linear_residual(o, w_out, x, w_lead=j, k=w_out.shape[1], name="sb_out")


def swiglu_hidden(h, w_gu, w_lead, name):
    d = h.shape[1]
    f = w_gu.shape[-1] // 2
    tm, tn = TM_WIDE, TN
    return matmul(h, w_gu, w_lead=w_lead, kb_x=0, kb_w=0, k=d, part_blocks=(0, f // tn),
                  tn=tn, tm=tm, n_out=f, out_dtype=BF16, epilogue=_ep_swiglu, name=name)


def dense_swiglu(h, x, w_gu, w_down, f_idx):
    k_split = w_gu.shape[1]
    a = swiglu_hidden(h, w_gu, f_idx, "dense_gu")
    for kb in range(w_down.shape[1] // k_split):
        x = linear_residual(a, w_down, x, w_lead=f_idx, kb_x=kb, kb_w=kb, k=k_split,
                            name="dense_down")
    return x


def moe_routing(idx2, n_exp, tmg, n_col_tiles):
    m = idx2.shape[0]
    n_slots = 2 * m
    t_bound = n_slots // tmg + n_exp
    flat = idx2.reshape(-1)
    onehot = (flat[:, None] == jnp.arange(n_exp, dtype=jnp.int32)[None, :]).astype(jnp.int32)
    rank = jnp.sum((jnp.cumsum(onehot, axis=0) - onehot) * onehot, axis=1)
    counts = jnp.sum(onehot, axis=0)
    tiles = (counts + tmg - 1) // tmg
    tile_end = jnp.cumsum(tiles)
    tile_start = tile_end - tiles
    used = tile_end[-1]
    pos = tile_start[flat] * tmg + rank
    row_token = jnp.zeros((t_bound * tmg,), jnp.int32).at[pos].set(
        jnp.arange(n_slots, dtype=jnp.int32) // 2)

    def work_list(nt):
        item_end = jnp.cumsum(tiles * nt)
        total = item_end[-1]
        w = jnp.arange(nt * t_bound, dtype=jnp.int32)
        valid = w < total
        wc = jnp.minimum(w, total - 1)
        e = jnp.sum((item_end[None, :] <= wc[:, None]).astype(jnp.int32), axis=1)
        local = wc - (item_end - tiles * nt)[e]
        r = jnp.maximum(tiles[e], 1)
        n = local // r
        row = tile_start[e] + local % r
        first = valid & (local % r == 0)
        k = w - total
        out_row = jnp.where(valid, row, used + k // nt)
        out_col = jnp.where(valid, n, k % nt)
        return jnp.stack([e, n, row, out_row, out_col, valid.astype(jnp.int32),
                          first.astype(jnp.int32)]).astype(jnp.int32)

    return pos, row_token, [work_list(nt) for nt in n_col_tiles]


GATHER_UNROLL = 8
COMBINE_ROWS = 16


def _issue_row_gather(n_rows, src_index, src_hbm, dst_ref, sem):
    def body(group, _):
        for u in range(GATHER_UNROLL):
            r = GATHER_UNROLL * group + u
            pltpu.make_async_copy(src_hbm.at[pl.ds(src_index(r), 1)],
                                  dst_ref.at[pl.ds(r, 1)], sem).start(priority=u % 2)
        return 0
    lax.fori_loop(0, n_rows // GATHER_UNROLL, body, 0)


def _wait_row_gather(n_rows, src_hbm, dst_ref, sem):
    pltpu.make_async_copy(src_hbm.at[pl.ds(0, n_rows)], dst_ref, sem).wait()


def _gather_unpack_kernel(tok_ref, hp_hbm, o_ref, buf, sems, *, tg):
    i = pl.program_id(0)
    slot = i % 2

    def issue(step, s):
        _issue_row_gather(tg, lambda r: tok_ref[step * tg + r], hp_hbm, buf.at[s], sems.at[s])

    @pl.when(i == 0)
    def _prime():
        issue(0, 0)

    @pl.when(i + 1 < pl.num_programs(0))
    def _prefetch():
        issue(i + 1, 1 - slot)

    _wait_row_gather(tg, hp_hbm, buf.at[slot], sems.at[slot])
    packed = buf[slot]
    half = packed.shape[1]
    o_ref[:, :half] = pltpu.bitcast(packed << 16, F32).astype(o_ref.dtype)
    o_ref[:, half:] = pltpu.bitcast(packed & jnp.uint32(0xFFFF0000), F32).astype(o_ref.dtype)


def gather_rows(hp, row_token, tg=256):
    half = hp.shape[1]
    p = row_token.shape[0]
    grid_spec = pltpu.PrefetchScalarGridSpec(
        num_scalar_prefetch=1,
        grid=(p // tg,),
        in_specs=[pl.BlockSpec(memory_space=pl.ANY)],
        out_specs=pl.BlockSpec((tg, 2 * half), lambda i, tok: (i, 0)),
        scratch_shapes=[pltpu.VMEM((2, tg, half), jnp.uint32), pltpu.SemaphoreType.DMA((2,))])
    return pl.pallas_call(
        functools.partial(_gather_unpack_kernel, tg=tg),
        grid_spec=grid_spec,
        out_shape=jax.ShapeDtypeStruct((p, 2 * half), BF16),
        compiler_params=_params(1),
        name="moe_gather",
    )(row_token, hp)


def _grouped_mm_kernel(work_ref, x_ref, *refs, n_parts, epilogue):
    w_refs = refs[:n_parts]
    o_ref = refs[n_parts]
    wb_ref = refs[n_parts + 1]
    w = pl.program_id(0)

    @pl.when(work_ref[6, w] == 1)
    def _cast_weights():
        for p in range(n_parts):
            wb_ref[p] = w_refs[p][...].astype(BF16)

    @pl.when(work_ref[5, w] == 1)
    def _compute():
        x = x_ref[...]
        accs = [jnp.dot(x, wb_ref[p], preferred_element_type=F32) for p in range(n_parts)]
        epilogue(accs, (), o_ref, ())

    @pl.when(work_ref[5, w] == 0)
    def _zero_fill():
        o_ref[...] = jnp.zeros_like(o_ref)


def grouped_matmul(x, w, work, *, lead_base, k, part_blocks, tn, tmg, n_out, out_dtype,
                   epilogue, name):
    p = x.shape[0]
    n_parts = len(part_blocks)
    in_specs = [pl.BlockSpec((tmg, k), lambda i, wk: (wk[2, i], 0))]
    for pb in part_blocks:
        in_specs.append(pl.BlockSpec(
            (None, k, tn), lambda i, wk, pb=pb: (lead_base + wk[0, i], 0, pb + wk[1, i])))
    grid_spec = pltpu.PrefetchScalarGridSpec(
        num_scalar_prefetch=1,
        grid=(work.shape[1],),
        in_specs=in_specs,
        out_specs=pl.BlockSpec((tmg, tn), lambda i, wk: (wk[3, i], wk[4, i])),
        scratch_shapes=[pltpu.VMEM((n_parts, k, tn), BF16)])
    return pl.pallas_call(
        functools.partial(_grouped_mm_kernel, n_parts=n_parts, epilogue=epilogue),
        grid_spec=grid_spec,
        out_shape=jax.ShapeDtypeStruct((p, n_out), out_dtype),
        compiler_params=_params(1),
        name=name,
    )(work, x, *([w] * n_parts))


def _combine_kernel(pos_ref, x_ref, gate_ref, g_ref, y_hbm, *refs, tc, emit_x):
    if emit_x:
        o_ref, h_ref, buf, sems = refs
    else:
        h_ref, buf, sems = refs
    i = pl.program_id(0)
    slot = i % 2

    def issue(step, s):
        for k in range(2):
            _issue_row_gather(tc, lambda r, k=k: pos_ref[2 * (step * tc + r) + k], y_hbm,
                              buf.at[s, k], sems.at[s, k])

    @pl.when(i == 0)
    def _prime():
        issue(0, 0)

    @pl.when(i + 1 < pl.num_programs(0))
    def _prefetch():
        issue(i + 1, 1 - slot)

    for k in range(2):
        _wait_row_gather(tc, y_hbm, buf.at[slot, k], sems.at[slot, k])
    def chunk(c, _):
        rows = pl.ds(pl.multiple_of(c * COMBINE_ROWS, COMBINE_ROWS), COMBINE_ROWS)
        gates = gate_ref[rows, :]
        x = (x_ref[rows, :] + gates[:, 0:1] * buf[slot, 0, rows, :]
             + gates[:, 1:2] * buf[slot, 1, rows, :])
        if emit_x:
            o_ref[rows, :] = x
        ms = jnp.mean(x * x, axis=-1, keepdims=True)
        h_ref[rows, :] = (x * lax.rsqrt(ms + NORM_EPS) * g_ref[...]).astype(h_ref.dtype)
        return 0

    lax.fori_loop(0, tc // COMBINE_ROWS, chunk, 0, unroll=4)


def moe_combine(x, gates, y, pos, next_g, emit_x, tc=256):
    m, d = x.shape
    row_spec = pl.BlockSpec((tc, d), lambda i, pos: (i, 0))
    out_specs = [row_spec, row_spec] if emit_x else row_spec
    out_shape = ([jax.ShapeDtypeStruct((m, d), F32), jax.ShapeDtypeStruct((m, d), BF16)]
                 if emit_x else jax.ShapeDtypeStruct((m, d), F32))
    grid_spec = pltpu.PrefetchScalarGridSpec(
        num_scalar_prefetch=1,
        grid=(m // tc,),
        in_specs=[row_spec,
                  pl.BlockSpec((tc, LANES), lambda i, pos: (i, 0)),
                  pl.BlockSpec((1, d), lambda i, pos: (0, 0)),
                  pl.BlockSpec(memory_space=pl.ANY)],
        out_specs=out_specs,
        scratch_shapes=[pltpu.VMEM((2, 2, tc, d), F32), pltpu.SemaphoreType.DMA((2, 2))])
    return pl.pallas_call(
        functools.partial(_combine_kernel, tc=tc, emit_x=emit_x),
        grid_spec=grid_spec,
        out_shape=out_shape,
        compiler_params=_params(1),
        name="moe_combine",
    )(pos, x, gates, next_g.reshape(1, d), y)


def moe_swiglu(x, norm_g, w_router, w_gu, w_down, f_idx, next_g, emit_x):
    n_exp, d, two_f = w_gu.shape[1:]
    d_exp = two_f // 2
    tmg, tn_gu, tn_down = TMG, TN, min(d, 4 * TN)
    gates, idx, hp = router(x, norm_g, w_router)
    pos, row_token, (work_gu, work_down) = moe_routing(
        idx[:, :2], n_exp, tmg, (d_exp // tn_gu, d // tn_down))
    hs = gather_rows(hp, row_token)
    act = grouped_matmul(hs, w_gu.reshape(-1, d, two_f), work_gu, lead_base=f_idx * n_exp, k=d,
                         part_blocks=(0, d_exp // tn_gu), tn=tn_gu, tmg=tmg, n_out=d_exp,
                         out_dtype=BF16, epilogue=_ep_swiglu, name="moe_gu")
    y = grouped_matmul(act, w_down.reshape(-1, d_exp, d), work_down, lead_base=f_idx * n_exp,
                       k=d_exp, part_blocks=(0,), tn=tn_down, tmg=tmg, n_out=d, out_dtype=F32,
                       epilogue=_ep_store, name="moe_down")
    return moe_combine(x, gates, y, pos, next_g, emit_x)


def kernel(x, mix_norm, ffn_norm, conv_w_in, conv_w, conv_w_out, gmlp_w_in, gmlp_v_norm_g,
           gmlp_v_norm_b, gmlp_w_s, gmlp_b_s, gmlp_w_out, sb_w_qkv, sb_w_out, dense_w_gu,
           dense_w_down, moe_w_router, moe_w_gu, moe_w_down, final_norm):
    batch, seq, d = x.shape
    depth = mix_norm.shape[0]
    x = x.reshape(batch * seq, d)
    h = None
    for i in range(depth):
        if h is None:
            h = rmsnorm(x, mix_norm[i], BF16)
        kind, j = i % 3, i // 3
        if kind == 0:
            x = short_conv_mixer(h, x, conv_w_in, conv_w, conv_w_out, j, seq)
        elif kind == 1:
            x = chunked_gmlp_mixer(h, x, gmlp_w_in, gmlp_v_norm_g, gmlp_v_norm_b, gmlp_w_s,
                                   gmlp_b_s, gmlp_w_out, j)
        else:
            x = stick_breaking_mixer(h, x, sb_w_qkv, sb_w_out, j, batch, seq)
        f = i // 2
        last = i == depth - 1
        if i % 2 == 0:
            h = rmsnorm(x, ffn_norm[i], BF16)
            x = dense_swiglu(h, x, dense_w_gu, dense_w_down, f)
            h = None
        elif last:
            out = moe_swiglu(x, ffn_norm[i], moe_w_router[f], moe_w_gu, moe_w_down, f,
                             final_norm, emit_x=False)
            return out.reshape(batch, seq, d)
        else:
            x, h = moe_swiglu(x, ffn_norm[i], moe_w_router[f], moe_w_gu, moe_w_down, f,
                              mix_norm[i + 1], emit_x=True)
    return rmsnorm(x, final_norm, F32).reshape(batch, seq, d)
```

```python
import functools

import jax
import jax.numpy as jnp
from jax import lax
from jax.experimental import pallas as pl
from jax.experimental.pallas import tpu as pltpu

F32 = jnp.float32
BF16 = jnp.bfloat16

NORM_EPS = 1e-5
CONV_WIDTH = 3
CHUNK = 128
GROUP_DIM = 128
HEAD_DIM = 128
N_EXPERTS = 8
LANES = 128
VMEM_LIMIT = 61 * 1024 * 1024
SQRT_HALF = 0.7071067811865476
LOG2_E = 1.4426950408889634

TM = 1024
TN_PART = 256
TM_WIDE = 512
TN_WIDE = 1024
TN = 512
TQ = 256
SB_HEADS_PER_STEP = 4
TMG = 256


def _params(n_axes):
    return pltpu.CompilerParams(
        dimension_semantics=("arbitrary",) * n_axes, vmem_limit_bytes=VMEM_LIMIT)


def _rmsnorm_kernel(x_ref, g_ref, o_ref):
    x = x_ref[...]
    ms = jnp.mean(x * x, axis=-1, keepdims=True)
    o_ref[...] = (x * lax.rsqrt(ms + NORM_EPS) * g_ref[...]).astype(o_ref.dtype)


def rmsnorm(x, g, out_dtype, tr=256):
    m, d = x.shape
    return pl.pallas_call(
        _rmsnorm_kernel,
        grid=(m // tr,),
        in_specs=[pl.BlockSpec((tr, d), lambda i: (i, 0)),
                  pl.BlockSpec((1, d), lambda i: (0, 0))],
        out_specs=pl.BlockSpec((tr, d), lambda i: (i, 0)),
        out_shape=jax.ShapeDtypeStruct((m, d), out_dtype),
        compiler_params=_params(1),
        name="rmsnorm",
    )(x, g.reshape(1, d))


def _router_kernel(x_ref, g_ref, wr_ref, gate_ref, idx_ref, hp_ref):
    x = x_ref[...]
    ms = jnp.mean(x * x, axis=-1, keepdims=True)
    h = x * lax.rsqrt(ms + NORM_EPS) * g_ref[...]
    half = h.shape[1] // 2
    bits = pltpu.bitcast(h.astype(BF16).astype(F32), jnp.uint32)
    hp_ref[...] = (bits[:, :half] >> 16) | bits[:, half:]
    h_hi = h.astype(BF16)
    h_lo = (h - h_hi.astype(F32)).astype(BF16)
    logits = (jnp.dot(h_hi, wr_ref[0], preferred_element_type=F32)
              + jnp.dot(h_lo, wr_ref[0], preferred_element_type=F32)
              + jnp.dot(h_hi, wr_ref[1], preferred_element_type=F32))
    lane = lax.broadcasted_iota(jnp.int32, logits.shape, 1)
    neg = jnp.float32(-jnp.inf)
    logits = jnp.where(lane < N_EXPERTS, logits, neg)
    v1 = jnp.max(logits, axis=-1, keepdims=True)
    i1 = jnp.min(jnp.where(logits == v1, lane, LANES), axis=-1, keepdims=True)
    rest = jnp.where(lane == i1, neg, logits)
    v2 = jnp.max(rest, axis=-1, keepdims=True)
    i2 = jnp.min(jnp.where(rest == v2, lane, LANES), axis=-1, keepdims=True)
    e2 = jnp.exp(v2 - v1)
    denom = 1.0 + e2
    g1 = 1.0 / denom
    g2 = e2 / denom
    gate_ref[...] = jnp.where(lane == 0, g1, jnp.where(lane == 1, g2, 0.0))
    idx_ref[...] = jnp.where(lane == 0, i1, jnp.where(lane == 1, i2, 0))


def router(x, g, w_router, tr=256):
    m, d = x.shape
    e = w_router.shape[1]
    wr = jnp.zeros((d, LANES), F32).at[:, :e].set(w_router)
    wr_hi = wr.astype(BF16)
    wr = jnp.stack([wr_hi, (wr - wr_hi.astype(F32)).astype(BF16)])
    return pl.pallas_call(
        _router_kernel,
        grid=(m // tr,),
        in_specs=[pl.BlockSpec((tr, d), lambda i: (i, 0)),
                  pl.BlockSpec((1, d), lambda i: (0, 0)),
                  pl.BlockSpec((2, d, LANES), lambda i: (0, 0, 0))],
        out_specs=[pl.BlockSpec((tr, LANES), lambda i: (i, 0)),
                   pl.BlockSpec((tr, LANES), lambda i: (i, 0)),
                   pl.BlockSpec((tr, d // 2), lambda i: (i, 0))],
        out_shape=[jax.ShapeDtypeStruct((m, LANES), F32),
                   jax.ShapeDtypeStruct((m, LANES), jnp.int32),
                   jax.ShapeDtypeStruct((m, d // 2), jnp.uint32)],
        compiler_params=_params(1),
        name="router",
    )(x, g.reshape(1, d), wr)


def _mm_kernel(*refs, n_parts, n_extra, n_scratch, epilogue):
    x_ref = refs[0]
    w_refs = refs[1:1 + n_parts]
    extra_refs = refs[1 + n_parts:1 + n_parts + n_extra]
    o_ref = refs[1 + n_parts + n_extra]
    wb_ref = refs[2 + n_parts + n_extra]
    scratch = refs[3 + n_parts + n_extra:]
    assert len(scratch) == n_scratch

    @pl.when(pl.program_id(1) == 0)
    def _cast_weights():
        for p in range(n_parts):
            wb_ref[p] = w_refs[p][...].astype(BF16)

    x = x_ref[...]
    accs = [jnp.dot(x, wb_ref[p], preferred_element_type=F32) for p in range(n_parts)]
    epilogue(accs, extra_refs, o_ref, scratch)


def matmul(x, w, *, w_lead, kb_x, kb_w, k, part_blocks, tn, tm, n_out, out_dtype,
           epilogue, extras=(), extra_specs=(), scratch_shapes=(), name):
    m = x.shape[0]
    n_parts = len(part_blocks)
    grid = (n_out // tn, m // tm)
    in_specs = [pl.BlockSpec((tm, k), lambda n, i: (i, kb_x))]
    for pb in part_blocks:
        in_specs.append(pl.BlockSpec((None, k, tn),
                                     lambda n, i, pb=pb: (w_lead, kb_w, pb + n)))
    in_specs.extend(extra_specs)
    kern = functools.partial(_mm_kernel, n_parts=n_parts, n_extra=len(extras),
                             n_scratch=len(scratch_shapes), epilogue=epilogue)
    return pl.pallas_call(
        kern,
        grid=grid,
        in_specs=in_specs,
        out_specs=pl.BlockSpec((tm, tn), lambda n, i: (i, n)),
        out_shape=jax.ShapeDtypeStruct((m, n_out), out_dtype),
        scratch_shapes=[pltpu.VMEM((n_parts, k, tn), BF16), *scratch_shapes],
        compiler_params=_params(2),
        name=name,
    )(x, *([w] * n_parts), *extras)


def _ep_store(accs, extra_refs, o_ref, scratch):
    o_ref[...] = accs[0].astype(o_ref.dtype)


def _ep_residual(accs, extra_refs, o_ref, scratch):
    o_ref[...] = extra_refs[0][...] + accs[0]


def _ep_swiglu(accs, extra_refs, o_ref, scratch):
    g, u = accs
    o_ref[...] = (g * jax.nn.sigmoid(g) * u).astype(o_ref.dtype)


def _ep_gelu(accs, extra_refs, o_ref, scratch):
    a = accs[0]
    o_ref[...] = (0.5 * a * (1.0 + lax.erf(a * SQRT_HALF))).astype(o_ref.dtype)


def _ep_conv(accs, extra_refs, o_ref, scratch, *, tiles_per_seq):
    b, c, xin = accs
    cw_ref, = extra_refs
    tail_ref, = scratch
    z = c * xin
    tm = z.shape[0]
    i = pl.program_id(1)

    @pl.when(i % tiles_per_seq == 0)
    def _zero_tail():
        tail_ref[...] = jnp.zeros_like(tail_ref)

    row = lax.broadcasted_iota(jnp.int32, z.shape, 0)
    t1 = tail_ref[7:8, :]
    t2 = tail_ref[6:7, :]
    z1 = jnp.where(row == 0, t1, pltpu.roll(z, 1, axis=0))
    z2 = jnp.where(row == 0, t2, jnp.where(row == 1, t1, pltpu.roll(z, 2, axis=0)))
    zc = cw_ref[0:1, :] * z2 + cw_ref[1:2, :] * z1 + cw_ref[2:3, :] * z
    o_ref[...] = (b * zc).astype(o_ref.dtype)
    tail_ref[...] = z[tm - 8:, :]


def _res_spec(tm, tn):
    return pl.BlockSpec((tm, tn), lambda n, i: (i, n))


def linear_residual(a, w, res, *, w_lead, kb_x=0, kb_w=0, k, name):
    n_out = w.shape[-1]
    tm, tn = TM_WIDE, TN_WIDE
    return matmul(a, w, w_lead=w_lead, kb_x=kb_x, kb_w=kb_w, k=k, part_blocks=(0,), tn=tn,
                  tm=tm, n_out=n_out, out_dtype=F32, epilogue=_ep_residual,
                  extras=(res,), extra_specs=(_res_spec(tm, tn),), name=name)


def short_conv_mixer(h, x, w_in, conv_w, w_out, j, seq):
    d = h.shape[1]
    tm, tn = min(TM, seq), TN_PART
    nb = d // tn
    cw = jnp.zeros((8, d), F32).at[:CONV_WIDTH].set(conv_w[j])
    gated = matmul(
        h, w_in, w_lead=j, kb_x=0, kb_w=0, k=d, part_blocks=(0, nb, 2 * nb), tn=tn, tm=tm,
        n_out=d, out_dtype=BF16,
        epilogue=functools.partial(_ep_conv, tiles_per_seq=seq // tm),
        extras=(cw,), extra_specs=(pl.BlockSpec((8, tn), lambda n, i: (0, n)),),
        scratch_shapes=(pltpu.VMEM((8, tn), F32),), name="conv_in")
    return linear_residual(gated, w_out, x, w_lead=j, k=d, name="conv_out")


def _gmlp_spatial_kernel(u_ref, v_ref, g_ref, b_ref, ws_ref, bst_ref, o_ref):
    v = v_ref[...]
    mu = jnp.mean(v, axis=-1, keepdims=True)
    vc = v - mu
    var = jnp.mean(vc * vc, axis=-1, keepdims=True)
    vn = (vc * lax.rsqrt(var + NORM_EPS) * g_ref[...] + b_ref[...]).astype(BF16)
    t_idx = lax.broadcasted_iota(jnp.int32, (CHUNK, CHUNK), 0)
    s_idx = lax.broadcasted_iota(jnp.int32, (CHUNK, CHUNK), 1)
    causal = s_idx <= t_idx
    n_groups = ws_ref.shape[0]
    for g in range(n_groups):
        sl = slice(g * GROUP_DIM, (g + 1) * GROUP_DIM)
        wm = jnp.where(causal, ws_ref[g], 0.0).astype(BF16)
        sv = jnp.dot(wm, vn[:, sl], preferred_element_type=F32) + bst_ref[:, g:g + 1]
        o_ref[:, sl] = (u_ref[:, sl] * sv).astype(o_ref.dtype)


def gmlp_spatial(uv, v_norm_g, v_norm_b, w_s, b_s):
    m = uv.shape[0]
    inner = uv.shape[1] // 2
    n_groups = inner // GROUP_DIM
    return pl.pallas_call(
        _gmlp_spatial_kernel,
        grid=(m // CHUNK,),
        in_specs=[pl.BlockSpec((CHUNK, inner), lambda i: (i, 0)),
                  pl.BlockSpec((CHUNK, inner), lambda i: (i, 1)),
                  pl.BlockSpec((1, inner), lambda i: (0, 0)),
                  pl.BlockSpec((1, inner), lambda i: (0, 0)),
                  pl.BlockSpec((n_groups, CHUNK, CHUNK), lambda i: (0, 0, 0)),
                  pl.BlockSpec((CHUNK, n_groups), lambda i: (0, 0))],
        out_specs=pl.BlockSpec((CHUNK, inner), lambda i: (i, 0)),
        out_shape=jax.ShapeDtypeStruct((m, inner), BF16),
        compiler_params=_params(1),
        name="gmlp_spatial",
    )(uv, uv, v_norm_g.reshape(1, inner), v_norm_b.reshape(1, inner), w_s, b_s.T)


def chunked_gmlp_mixer(h, x, w_in, v_norm_g, v_norm_b, w_s, b_s, w_out, j):
    d = h.shape[1]
    inner2 = w_in.shape[-1]
    uv = matmul(h, w_in, w_lead=j, kb_x=0, kb_w=0, k=d, part_blocks=(0,), tn=TN_WIDE,
                tm=TM_WIDE, n_out=inner2, out_dtype=F32, epilogue=_ep_gelu, name="gmlp_in")
    gated = gmlp_spatial(uv, v_norm_g[j], v_norm_b[j], w_s[j], b_s[j])
    return linear_residual(gated, w_out, x, w_lead=j, k=inner2 // 2, name="gmlp_out")


def _sb_attn_kernel(q_ref, k_ref, v_ref, o_ref, z_buf, hl_buf, acc_buf, carry_buf,
                    *, tq, scale):
    seq = q_ref.shape[0]
    n_heads = q_ref.shape[1] // HEAD_DIM
    n_blk = seq // tq
    r = lax.broadcasted_iota(jnp.int32, (tq, tq), 0)
    c = lax.broadcasted_iota(jnp.int32, (tq, tq), 1)
    strict = c < r
    r2 = lax.broadcasted_iota(jnp.int32, (2 * tq, tq), 0)
    c2 = lax.broadcasted_iota(jnp.int32, (2 * tq, tq), 1)
    suffix2 = (jnp.where(r2 >= tq, r2 - tq, r2) >= c2).astype(BF16)

    def head(hd):
        return slice(hd * HEAD_DIM, (hd + 1) * HEAD_DIM)

    def scores(qb, kb):
        z2 = lax.dot_general(qb, kb, (((1,), (1,)), ((), ())), preferred_element_type=F32)
        z2 = z2 * (scale * LOG2_E)
        sp = jnp.maximum(z2, 0.0) + jnp.log2(1.0 + jnp.exp2(-jnp.abs(z2)))
        return z2, sp

    def split(sp):
        hi = sp.astype(BF16)
        return hi, (sp - hi.astype(F32)).astype(BF16)

    heads = range(n_heads)

    def score_head(hd, slot, q0, k0, diagonal):
        z, sp = scores(q_ref[pl.ds(q0, tq), head(hd)], k_ref[pl.ds(k0, tq), head(hd)])
        hi, lo = split(jnp.where(strict, sp, 0.0) if diagonal else sp)
        z_buf[slot, hd] = z
        hl_buf[slot, hd, :, :tq] = hi
        hl_buf[slot, hd, :, tq:] = lo

    def weights_head(hd, slot, diagonal):
        s = jnp.dot(hl_buf[slot, hd], suffix2, preferred_element_type=F32)
        if diagonal:
            att = jnp.where(strict, jnp.exp2(z_buf[slot, hd] - s), 0.0)
            carry_buf[hd] = s[:, 0:1]
        else:
            att = jnp.exp2(z_buf[slot, hd] - s - carry_buf[hd])
            carry_buf[hd] += s[:, 0:1]
        return att.astype(BF16)

    def values_head(hd, att, k0, diagonal):
        pv = jnp.dot(att, v_ref[pl.ds(k0, tq), head(hd)], preferred_element_type=F32)
        if diagonal:
            acc_buf[hd] = pv
        else:
            acc_buf[hd] += pv

    def score_stage(slot, q0, k0, diagonal):
        for hd in heads:
            score_head(hd, slot, q0, k0, diagonal)

    def value_stage(slot, k0, diagonal):
        atts = [weights_head(hd, slot, diagonal) for hd in heads]
        for hd in heads:
            values_head(hd, atts[hd], k0, diagonal)

    def write_out(q0):
        for hd in heads:
            o_ref[pl.ds(q0, tq), head(hd)] = acc_buf[hd].astype(o_ref.dtype)

    score_stage(0, 0, 0, True)
    value_stage(0, 0, True)
    write_out(0)

    def q_block(qi, _):
        q0 = pl.multiple_of(qi * tq, tq)
        score_stage(1, q0, q0, True)
        k_first = pl.multiple_of((qi - 1) * tq, tq)
        for hd in heads:
            score_head(hd, 0, q0, k_first, False)
            values_head(hd, weights_head(hd, 1, True), q0, True)

        def kv_block(t, _):
            k_cur = pl.multiple_of((qi - t) * tq, tq)
            k_next = pl.multiple_of((qi - t - 1) * tq, tq)
            for cur in range(2):
                @pl.when((t - 1) % 2 == cur)
                def _step():
                    for hd in heads:
                        score_head(hd, 1 - cur, q0, k_next, False)
                        values_head(hd, weights_head(hd, cur, False), k_cur, False)
            return 0

        lax.fori_loop(1, qi, kv_block, 0)
        for cur in range(2):
            @pl.when((qi - 1) % 2 == cur)
            def _last():
                value_stage(cur, 0, False)
        write_out(q0)
        return 0

    lax.fori_loop(1, n_blk, q_block, 0)


def sb_attention(qkv, batch, seq):
    m, three_d = qkv.shape
    n_heads = three_d // (3 * HEAD_DIM)
    hb = min(SB_HEADS_PER_STEP, n_heads)
    n_groups = n_heads // hb
    width = hb * HEAD_DIM
    kern = functools.partial(_sb_attn_kernel, tq=TQ, scale=HEAD_DIM ** -0.5)
    return pl.pallas_call(
        kern,
        grid=(batch, n_groups),
        in_specs=[pl.BlockSpec((seq, width), lambda b, h: (b, h)),
                  pl.BlockSpec((seq, width), lambda b, h: (b, n_groups + h)),
                  pl.BlockSpec((seq, width), lambda b, h: (b, 2 * n_groups + h))],
        out_specs=pl.BlockSpec((seq, width), lambda b, h: (b, h)),
        out_shape=jax.ShapeDtypeStruct((m, n_heads * HEAD_DIM), BF16),
        scratch_shapes=[pltpu.VMEM((2, hb, TQ, TQ), F32), pltpu.VMEM((2, hb, TQ, 2 * TQ), BF16),
                        pltpu.VMEM((hb, TQ, HEAD_DIM), F32),
                        pltpu.VMEM((hb, TQ, 1), F32)],
        compiler_params=_params(2),
        name="sb_attention",
    )(qkv, qkv, qkv)


def stick_breaking_mixer(h, x, w_qkv, w_out, j, batch, seq):
    d = h.shape[1]
    qkv = matmul(h, w_qkv, w_lead=j, kb_x=0, kb_w=0, k=d, part_blocks=(0,), tn=TN_WIDE,
                 tm=TM_WIDE, n_out=w_qkv.shape[-1], out_dtype=BF16, epilogue=_ep_store, name="sb_qkv")
    o = sb_attention(qkv, batch, seq)
    return linear_residual(o, w_out, x, w_lead=j, k=w_out.shape[1], name="sb_out")


def swiglu_hidden(h, w_gu, w_lead, name):
    d = h.shape[1]
    f = w_gu.shape[-1] // 2
    tm, tn = 2 * TM, TN_PART
    return matmul(h, w_gu, w_lead=w_lead, kb_x=0, kb_w=0, k=d, part_blocks=(0, f // tn),
                  tn=tn, tm=tm, n_out=f, out_dtype=BF16, epilogue=_ep_swiglu, name=name)


def dense_swiglu(h, x, w_gu, w_down, f_idx):
    k_split = w_gu.shape[1]
    a = swiglu_hidden(h, w_gu, f_idx, "dense_gu")
    for kb in range(w_down.shape[1] // k_split):
        x = linear_residual(a, w_down, x, w_lead=f_idx, kb_x=kb, kb_w=kb, k=k_split,
                            name="dense_down")
    return x


def moe_routing(idx2, n_exp, tmg, n_col_tiles):
    m = idx2.shape[0]
    n_slots = 2 * m
    t_bound = n_slots // tmg + n_exp
    flat = idx2.reshape(-1)
    onehot = (flat[:, None] == jnp.arange(n_exp, dtype=jnp.int32)[None, :]).astype(jnp.int32)
    rank = jnp.sum((jnp.cumsum(onehot, axis=0) - onehot) * onehot, axis=1)
    counts = jnp.sum(onehot, axis=0)
    tiles = (counts + tmg - 1) // tmg
    tile_end = jnp.cumsum(tiles)
    tile_start = tile_end - tiles
    used = tile_end[-1]
    pos = tile_start[flat] * tmg + rank
    row_token = jnp.zeros((t_bound * tmg,), jnp.int32).at[pos].set(
        jnp.arange(n_slots, dtype=jnp.int32) // 2)

    def work_list(nt):
        item_end = jnp.cumsum(tiles * nt)
        total = item_end[-1]
        w = jnp.arange(nt * t_bound, dtype=jnp.int32)
        valid = w < total
        wc = jnp.minimum(w, total - 1)
        e = jnp.sum((item_end[None, :] <= wc[:, None]).astype(jnp.int32), axis=1)
        local = wc - (item_end - tiles * nt)[e]
        r = jnp.maximum(tiles[e], 1)
        n = local // r
        row = tile_start[e] + local % r
        first = valid & (local % r == 0)
        k = w - total
        out_row = jnp.where(valid, row, used + k // nt)
        out_col = jnp.where(valid, n, k % nt)
        return jnp.stack([e, n, row, out_row, out_col, valid.astype(jnp.int32),
                          first.astype(jnp.int32)]).astype(jnp.int32)

    return pos, row_token, [work_list(nt) for nt in n_col_tiles]


GATHER_UNROLL = 8
COMBINE_ROWS = 16


def _issue_row_gather(n_rows, src_index, src_hbm, dst_ref, sem):
    def body(group, _):
        for u in range(GATHER_UNROLL):
            r = GATHER_UNROLL * group + u
            pltpu.make_async_copy(src_hbm.at[pl.ds(src_index(r), 1)],
                                  dst_ref.at[pl.ds(r, 1)], sem).start(priority=u % 2)
        return 0
    lax.fori_loop(0, n_rows // GATHER_UNROLL, body, 0)


def _wait_row_gather(n_rows, src_hbm, dst_ref, sem):
    pltpu.make_async_copy(src_hbm.at[pl.ds(0, n_rows)], dst_ref, sem).wait()


def _gather_unpack_kernel(tok_ref, hp_hbm, o_ref, buf, sems, *, tg):
    i = pl.program_id(0)
    slot = i % 2

    def issue(step, s):
        _issue_row_gather(tg, lambda r: tok_ref[step * tg + r], hp_hbm, buf.at[s], sems.at[s])

    @pl.when(i == 0)
    def _prime():
        issue(0, 0)

    @pl.when(i + 1 < pl.num_programs(0))
    def _prefetch():
        issue(i + 1, 1 - slot)

    _wait_row_gather(tg, hp_hbm, buf.at[slot], sems.at[slot])
    packed = buf[slot]
    half = packed.shape[1]
    o_ref[:, :half] = pltpu.bitcast(packed << 16, F32).astype(o_ref.dtype)
    o_ref[:, half:] = pltpu.bitcast(packed & jnp.uint32(0xFFFF0000), F32).astype(o_ref.dtype)


def gather_rows(hp, row_token, tg=256):
    half = hp.shape[1]
    p = row_token.shape[0]
    grid_spec = pltpu.PrefetchScalarGridSpec(
        num_scalar_prefetch=1,
        grid=(p // tg,),
        in_specs=[pl.BlockSpec(memory_space=pl.ANY)],
        out_specs=pl.BlockSpec((tg, 2 * half), lambda i, tok: (i, 0)),
        scratch_shapes=[pltpu.VMEM((2, tg, half), jnp.uint32), pltpu.SemaphoreType.DMA((2,))])
    return pl.pallas_call(
        functools.partial(_gather_unpack_kernel, tg=tg),
        grid_spec=grid_spec,
        out_shape=jax.ShapeDtypeStruct((p, 2 * half), BF16),
        compiler_params=_params(1),
        name="moe_gather",
    )(row_token, hp)


def _grouped_mm_kernel(work_ref, x_ref, *refs, n_parts, epilogue):
    w_refs = refs[:n_parts]
    o_ref = refs[n_parts]
    wb_ref = refs[n_parts + 1]
    w = pl.program_id(0)

    @pl.when(work_ref[6, w] == 1)
    def _cast_weights():
        for p in range(n_parts):
            wb_ref[p] = w_refs[p][...].astype(BF16)

    @pl.when(work_ref[5, w] == 1)
    def _compute():
        x = x_ref[...]
        accs = [jnp.dot(x, wb_ref[p], preferred_element_type=F32) for p in range(n_parts)]
        epilogue(accs, (), o_ref, ())

    @pl.when(work_ref[5, w] == 0)
    def _zero_fill():
        o_ref[...] = jnp.zeros_like(o_ref)


def grouped_matmul(x, w, work, *, lead_base, k, part_blocks, tn, tmg, n_out, out_dtype,
                   epilogue, name):
    p = x.shape[0]
    n_parts = len(part_blocks)
    in_specs = [pl.BlockSpec((tmg, k), lambda i, wk: (wk[2, i], 0))]
    for pb in part_blocks:
        in_specs.append(pl.BlockSpec(
            (None, k, tn), lambda i, wk, pb=pb: (lead_base + wk[0, i], 0, pb + wk[1, i])))
    grid_spec = pltpu.PrefetchScalarGridSpec(
        num_scalar_prefetch=1,
        grid=(work.shape[1],),
        in_specs=in_specs,
        out_specs=pl.BlockSpec((tmg, tn), lambda i, wk: (wk[3, i], wk[4, i])),
        scratch_shapes=[pltpu.VMEM((n_parts, k, tn), BF16)])
    return pl.pallas_call(
        functools.partial(_grouped_mm_kernel, n_parts=n_parts, epilogue=epilogue),
        grid_spec=grid_spec,
        out_shape=jax.ShapeDtypeStruct((p, n_out), out_dtype),
        compiler_params=_params(1),
        name=name,
    )(work, x, *([w] * n_parts))


def _combine_kernel(pos_ref, x_ref, gate_ref, g_ref, y_hbm, *refs, tc, emit_x):
    if emit_x:
        o_ref, h_ref, buf, sems = refs
    else:
        h_ref, buf, sems = refs
    i = pl.program_id(0)
    slot = i % 2

    def issue(step, s):
        for k in range(2):
            _issue_row_gather(tc, lambda r, k=k: pos_ref[2 * (step * tc + r) + k], y_hbm,
                              buf.at[s, k], sems.at[s, k])

    @pl.when(i == 0)
    def _prime():
        issue(0, 0)

    @pl.when(i + 1 < pl.num_programs(0))
    def _prefetch():
        issue(i + 1, 1 - slot)

    for k in range(2):
        _wait_row_gather(tc, y_hbm, buf.at[slot, k], sems.at[slot, k])
    def chunk(c, _):
        rows = pl.ds(pl.multiple_of(c * COMBINE_ROWS, COMBINE_ROWS), COMBINE_ROWS)
        gates = gate_ref[rows, :]
        x = (x_ref[rows, :] + gates[:, 0:1] * buf[slot, 0, rows, :]
             + gates[:, 1:2] * buf[slot, 1, rows, :])
        if emit_x:
            o_ref[rows, :] = x
        ms = jnp.mean(x * x, axis=-1, keepdims=True)
        h_ref[rows, :] = (x * lax.rsqrt(ms + NORM_EPS) * g_ref[...]).astype(h_ref.dtype)
        return 0

    lax.fori_loop(0, tc // COMBINE_ROWS, chunk, 0, unroll=4)


def moe_combine(x, gates, y, pos, next_g, emit_x, tc=256):
    m, d = x.shape
    row_spec = pl.BlockSpec((tc, d), lambda i, pos: (i, 0))
    out_specs = [row_spec, row_spec] if emit_x else row_spec
    out_shape = ([jax.ShapeDtypeStruct((m, d), F32), jax.ShapeDtypeStruct((m, d), BF16)]
                 if emit_x else jax.ShapeDtypeStruct((m, d), F32))
    grid_spec = pltpu.PrefetchScalarGridSpec(
        num_scalar_prefetch=1,
        grid=(m // tc,),
        in_specs=[row_spec,
                  pl.BlockSpec((tc, LANES), lambda i, pos: (i, 0)),
                  pl.BlockSpec((1, d), lambda i, pos: (0, 0)),
                  pl.BlockSpec(memory_space=pl.ANY)],
        out_specs=out_specs,
        scratch_shapes=[pltpu.VMEM((2, 2, tc, d), F32), pltpu.SemaphoreType.DMA((2, 2))])
    return pl.pallas_call(
        functools.partial(_combine_kernel, tc=tc, emit_x=emit_x),
        grid_spec=grid_spec,
        out_shape=out_shape,
        compiler_params=_params(1),
        name="moe_combine",
    )(pos, x, gates, next_g.reshape(1, d), y)


def moe_swiglu(x, norm_g, w_router, w_gu, w_down, f_idx, next_g, emit_x):
    n_exp, d, two_f = w_gu.shape[1:]
    d_exp = two_f // 2
    tmg, tn_gu, tn_down = TMG, TN, min(d, 4 * TN)
    gates, idx, hp = router(x, norm_g, w_router)
    pos, row_token, (work_gu, work_down) = moe_routing(
        idx[:, :2], n_exp, tmg, (d_exp // tn_gu, d // tn_down))
    hs = gather_rows(hp, row_token)
    act = grouped_matmul(hs, w_gu.reshape(-1, d, two_f), work_gu, lead_base=f_idx * n_exp, k=d,
                         part_blocks=(0, d_exp // tn_gu), tn=tn_gu, tmg=tmg, n_out=d_exp,
                         out_dtype=BF16, epilogue=_ep_swiglu, name="moe_gu")
    y = grouped_matmul(act, w_down.reshape(-1, d_exp, d), work_down, lead_base=f_idx * n_exp,
                       k=d_exp, part_blocks=(0,), tn=tn_down, tmg=tmg, n_out=d, out_dtype=F32,
                       epilogue=_ep_store, name="moe_down")
    return moe_combine(x, gates, y, pos, next_g, emit_x)


def kernel(x, mix_norm, ffn_norm, conv_w_in, conv_w, conv_w_out, gmlp_w_in, gmlp_v_norm_g,
           gmlp_v_norm_b, gmlp_w_s, gmlp_b_s, gmlp_w_out, sb_w_qkv, sb_w_out, dense_w_gu,
           dense_w_down, moe_w_router, moe_w_gu, moe_w_down, final_norm):
    batch, seq, d = x.shape
    depth = mix_norm.shape[0]
    x = x.reshape(batch * seq, d)
    h = None
    for i in range(depth):
        if h is None:
            h = rmsnorm(x, mix_norm[i], BF16)
        kind, j = i % 3, i // 3
        if kind == 0:
            x = short_conv_mixer(h, x, conv_w_in, conv_w, conv_w_out, j, seq)
        elif kind == 1:
            x = chunked_gmlp_mixer(h, x, gmlp_w_in, gmlp_v_norm_g, gmlp_v_norm_b, gmlp_w_s,
                                   gmlp_b_s, gmlp_w_out, j)
        else:
            x = stick_breaking_mixer(h, x, sb_w_qkv, sb_w_out, j, batch, seq)
        f = i // 2
        last = i == depth - 1
        if i % 2 == 0:
            h = rmsnorm(x, ffn_norm[i], BF16)
            x = dense_swiglu(h, x, dense_w_gu, dense_w_down, f)
            h = None
        elif last:
            out = moe_swiglu(x, ffn_norm[i], moe_w_router[f], moe_w_gu, moe_w_down, f,
                             final_norm, emit_x=False)
            return out.reshape(batch, seq, d)
        else:
            x, h = moe_swiglu(x, ffn_norm[i], moe_w_router[f], moe_w_gu, moe_w_down, f,
                              mix_norm[i + 1], emit_x=True)
    return rmsnorm(x, final_norm, F32).reshape(batch, seq, d)
```

```python
import functools

import jax
import jax.numpy as jnp
from jax import lax
from jax.experimental import pallas as pl
from jax.experimental.pallas import tpu as pltpu

F32 = jnp.float32
BF16 = jnp.bfloat16

NORM_EPS = 1e-5
CONV_WIDTH = 3
CHUNK = 128
GROUP_DIM = 128
HEAD_DIM = 128
N_EXPERTS = 8
LANES = 128
VMEM_LIMIT = 61 * 1024 * 1024
SQRT_HALF = 0.7071067811865476
LOG2_E = 1.4426950408889634

TM = 1024
TN_PART = 256
TM_WIDE = 512
TN_WIDE = 1024
TN = 512
TQ = 256
SB_HEADS_PER_STEP = 4
TMG = 512


def _params(n_axes):
    return pltpu.CompilerParams(
        dimension_semantics=("arbitrary",) * n_axes, vmem_limit_bytes=VMEM_LIMIT)


def _rmsnorm_kernel(x_ref, g_ref, o_ref):
    x = x_ref[...]
    ms = jnp.mean(x * x, axis=-1, keepdims=True)
    o_ref[...] = (x * lax.rsqrt(ms + NORM_EPS) * g_ref[...]).astype(o_ref.dtype)


def rmsnorm(x, g, out_dtype, tr=256):
    m, d = x.shape
    return pl.pallas_call(
        _rmsnorm_kernel,
        grid=(m // tr,),
        in_specs=[pl.BlockSpec((tr, d), lambda i: (i, 0)),
                  pl.BlockSpec((1, d), lambda i: (0, 0))],
        out_specs=pl.BlockSpec((tr, d), lambda i: (i, 0)),
        out_shape=jax.ShapeDtypeStruct((m, d), out_dtype),
        compiler_params=_params(1),
        name="rmsnorm",
    )(x, g.reshape(1, d))


def _router_kernel(x_ref, g_ref, wr_ref, gate_ref, idx_ref, hp_ref):
    x = x_ref[...]
    ms = jnp.mean(x * x, axis=-1, keepdims=True)
    h = x * lax.rsqrt(ms + NORM_EPS) * g_ref[...]
    half = h.shape[1] // 2
    bits = pltpu.bitcast(h.astype(BF16).astype(F32), jnp.uint32)
    hp_ref[...] = (bits[:, :half] >> 16) | bits[:, half:]
    h_hi = h.astype(BF16)
    h_lo = (h - h_hi.astype(F32)).astype(BF16)
    logits = (jnp.dot(h_hi, wr_ref[0], preferred_element_type=F32)
              + jnp.dot(h_lo, wr_ref[0], preferred_element_type=F32)
              + jnp.dot(h_hi, wr_ref[1], preferred_element_type=F32))
    lane = lax.broadcasted_iota(jnp.int32, logits.shape, 1)
    neg = jnp.float32(-jnp.inf)
    logits = jnp.where(lane < N_EXPERTS, logits, neg)
    v1 = jnp.max(logits, axis=-1, keepdims=True)
    i1 = jnp.min(jnp.where(logits == v1, lane, LANES), axis=-1, keepdims=True)
    rest = jnp.where(lane == i1, neg, logits)
    v2 = jnp.max(rest, axis=-1, keepdims=True)
    i2 = jnp.min(jnp.where(rest == v2, lane, LANES), axis=-1, keepdims=True)
    e2 = jnp.exp(v2 - v1)
    denom = 1.0 + e2
    g1 = 1.0 / denom
    g2 = e2 / denom
    gate_ref[...] = jnp.where(lane == 0, g1, jnp.where(lane == 1, g2, 0.0))
    idx_ref[...] = jnp.where(lane == 0, i1, jnp.where(lane == 1, i2, 0))


def router(x, g, w_router, tr=256):
    m, d = x.shape
    e = w_router.shape[1]
    wr = jnp.zeros((d, LANES), F32).at[:, :e].set(w_router)
    wr_hi = wr.astype(BF16)
    wr = jnp.stack([wr_hi, (wr - wr_hi.astype(F32)).astype(BF16)])
    return pl.pallas_call(
        _router_kernel,
        grid=(m // tr,),
        in_specs=[pl.BlockSpec((tr, d), lambda i: (i, 0)),
                  pl.BlockSpec((1, d), lambda i: (0, 0)),
                  pl.BlockSpec((2, d, LANES), lambda i: (0, 0, 0))],
        out_specs=[pl.BlockSpec((tr, LANES), lambda i: (i, 0)),
                   pl.BlockSpec((tr, LANES), lambda i: (i, 0)),
                   pl.BlockSpec((tr, d // 2), lambda i: (i, 0))],
        out_shape=[jax.ShapeDtypeStruct((m, LANES), F32),
                   jax.ShapeDtypeStruct((m, LANES), jnp.int32),
                   jax.ShapeDtypeStruct((m, d // 2), jnp.uint32)],
        compiler_params=_params(1),
        name="router",
    )(x, g.reshape(1, d), wr)


def _mm_kernel(*refs, n_parts, n_extra, n_scratch, epilogue):
    x_ref = refs[0]
    w_refs = refs[1:1 + n_parts]
    extra_refs = refs[1 + n_parts:1 + n_parts + n_extra]
    o_ref = refs[1 + n_parts + n_extra]
    wb_ref = refs[2 + n_parts + n_extra]
    scratch = refs[3 + n_parts + n_extra:]
    assert len(scratch) == n_scratch

    @pl.when(pl.program_id(1) == 0)
    def _cast_weights():
        for p in range(n_parts):
            wb_ref[p] = w_refs[p][...].astype(BF16)

    x = x_ref[...]
    accs = [jnp.dot(x, wb_ref[p], preferred_element_type=F32) for p in range(n_parts)]
    epilogue(accs, extra_refs, o_ref, scratch)


def matmul(x, w, *, w_lead, kb_x, kb_w, k, part_blocks, tn, tm, n_out, out_dtype,
           epilogue, extras=(), extra_specs=(), scratch_shapes=(), name):
    m = x.shape[0]
    n_parts = len(part_blocks)
    grid = (n_out // tn, m // tm)
    in_specs = [pl.BlockSpec((tm, k), lambda n, i: (i, kb_x))]
    for pb in part_blocks:
        in_specs.append(pl.BlockSpec((None, k, tn),
                                     lambda n, i, pb=pb: (w_lead, kb_w, pb + n)))
    in_specs.extend(extra_specs)
    kern = functools.partial(_mm_kernel, n_parts=n_parts, n_extra=len(extras),
                             n_scratch=len(scratch_shapes), epilogue=epilogue)
    return pl.pallas_call(
        kern,
        grid=grid,
        in_specs=in_specs,
        out_specs=pl.BlockSpec((tm, tn), lambda n, i: (i, n)),
        out_shape=jax.ShapeDtypeStruct((m, n_out), out_dtype),
        scratch_shapes=[pltpu.VMEM((n_parts, k, tn), BF16), *scratch_shapes],
        compiler_params=_params(2),
        name=name,
    )(x, *([w] * n_parts), *extras)


def _ep_store(accs, extra_refs, o_ref, scratch):
    o_ref[...] = accs[0].astype(o_ref.dtype)


def _ep_residual(accs, extra_refs, o_ref, scratch):
    o_ref[...] = extra_refs[0][...] + accs[0]


def _ep_swiglu(accs, extra_refs, o_ref, scratch):
    g, u = accs
    o_ref[...] = (g * jax.nn.sigmoid(g) * u).astype(o_ref.dtype)


def _ep_gelu(accs, extra_refs, o_ref, scratch):
    a = accs[0]
    o_ref[...] = (0.5 * a * (1.0 + lax.erf(a * SQRT_HALF))).astype(o_ref.dtype)


def _ep_conv(accs, extra_refs, o_ref, scratch, *, tiles_per_seq):
    b, c, xin = accs
    cw_ref, = extra_refs
    tail_ref, = scratch
    z = c * xin
    tm = z.shape[0]
    i = pl.program_id(1)

    @pl.when(i % tiles_per_seq == 0)
    def _zero_tail():
        tail_ref[...] = jnp.zeros_like(tail_ref)

    row = lax.broadcasted_iota(jnp.int32, z.shape, 0)
    t1 = tail_ref[7:8, :]
    t2 = tail_ref[6:7, :]
    z1 = jnp.where(row == 0, t1, pltpu.roll(z, 1, axis=0))
    z2 = jnp.where(row == 0, t2, jnp.where(row == 1, t1, pltpu.roll(z, 2, axis=0)))
    zc = cw_ref[0:1, :] * z2 + cw_ref[1:2, :] * z1 + cw_ref[2:3, :] * z
    o_ref[...] = (b * zc).astype(o_ref.dtype)
    tail_ref[...] = z[tm - 8:, :]


def _res_spec(tm, tn):
    return pl.BlockSpec((tm, tn), lambda n, i: (i, n))


def linear_residual(a, w, res, *, w_lead, kb_x=0, kb_w=0, k, name):
    n_out = w.shape[-1]
    tm, tn = TM_WIDE, TN_WIDE
    return matmul(a, w, w_lead=w_lead, kb_x=kb_x, kb_w=kb_w, k=k, part_blocks=(0,), tn=tn,
                  tm=tm, n_out=n_out, out_dtype=F32, epilogue=_ep_residual,
                  extras=(res,), extra_specs=(_res_spec(tm, tn),), name=name)


def short_conv_mixer(h, x, w_in, conv_w, w_out, j, seq):
    d = h.shape[1]
    tm, tn = min(TM, seq), TN_PART
    nb = d // tn
    cw = jnp.zeros((8, d), F32).at[:CONV_WIDTH].set(conv_w[j])
    gated = matmul(
        h, w_in, w_lead=j, kb_x=0, kb_w=0, k=d, part_blocks=(0, nb, 2 * nb), tn=tn, tm=tm,
        n_out=d, out_dtype=BF16,
        epilogue=functools.partial(_ep_conv, tiles_per_seq=seq // tm),
        extras=(cw,), extra_specs=(pl.BlockSpec((8, tn), lambda n, i: (0, n)),),
        scratch_shapes=(pltpu.VMEM((8, tn), F32),), name="conv_in")
    return linear_residual(gated, w_out, x, w_lead=j, k=d, name="conv_out")


def _gmlp_spatial_kernel(u_ref, v_ref, g_ref, b_ref, ws_ref, bst_ref, o_ref):
    v = v_ref[...]
    mu = jnp.mean(v, axis=-1, keepdims=True)
    vc = v - mu
    var = jnp.mean(vc * vc, axis=-1, keepdims=True)
    vn = (vc * lax.rsqrt(var + NORM_EPS) * g_ref[...] + b_ref[...]).astype(BF16)
    t_idx = lax.broadcasted_iota(jnp.int32, (CHUNK, CHUNK), 0)
    s_idx = lax.broadcasted_iota(jnp.int32, (CHUNK, CHUNK), 1)
    causal = s_idx <= t_idx
    n_groups = ws_ref.shape[0]
    for g in range(n_groups):
        sl = slice(g * GROUP_DIM, (g + 1) * GROUP_DIM)
        wm = jnp.where(causal, ws_ref[g], 0.0).astype(BF16)
        sv = jnp.dot(wm, vn[:, sl], preferred_element_type=F32) + bst_ref[:, g:g + 1]
        o_ref[:, sl] = (u_ref[:, sl] * sv).astype(o_ref.dtype)


def gmlp_spatial(uv, v_norm_g, v_norm_b, w_s, b_s):
    m = uv.shape[0]
    inner = uv.shape[1] // 2
    n_groups = inner // GROUP_DIM
    return pl.pallas_call(
        _gmlp_spatial_kernel,
        grid=(m // CHUNK,),
        in_specs=[pl.BlockSpec((CHUNK, inner), lambda i: (i, 0)),
                  pl.BlockSpec((CHUNK, inner), lambda i: (i, 1)),
                  pl.BlockSpec((1, inner), lambda i: (0, 0)),
                  pl.BlockSpec((1, inner), lambda i: (0, 0)),
                  pl.BlockSpec((n_groups, CHUNK, CHUNK), lambda i: (0, 0, 0)),
                  pl.BlockSpec((CHUNK, n_groups), lambda i: (0, 0))],
        out_specs=pl.BlockSpec((CHUNK, inner), lambda i: (i, 0)),
        out_shape=jax.ShapeDtypeStruct((m, inner), BF16),
        compiler_params=_params(1),
        name="gmlp_spatial",
    )(uv, uv, v_norm_g.reshape(1, inner), v_norm_b.reshape(1, inner), w_s, b_s.T)


def chunked_gmlp_mixer(h, x, w_in, v_norm_g, v_norm_b, w_s, b_s, w_out, j):
    d = h.shape[1]
    inner2 = w_in.shape[-1]
    uv = matmul(h, w_in, w_lead=j, kb_x=0, kb_w=0, k=d, part_blocks=(0,), tn=TN_WIDE,
                tm=TM_WIDE, n_out=inner2, out_dtype=F32, epilogue=_ep_gelu, name="gmlp_in")
    gated = gmlp_spatial(uv, v_norm_g[j], v_norm_b[j], w_s[j], b_s[j])
    return linear_residual(gated, w_out, x, w_lead=j, k=inner2 // 2, name="gmlp_out")


def _sb_attn_kernel(q_ref, k_ref, v_ref, o_ref, z_buf, hl_buf, acc_buf, carry_buf,
                    *, tq, scale):
    seq = q_ref.shape[0]
    n_heads = q_ref.shape[1] // HEAD_DIM
    n_blk = seq // tq
    r = lax.broadcasted_iota(jnp.int32, (tq, tq), 0)
    c = lax.broadcasted_iota(jnp.int32, (tq, tq), 1)
    strict = c < r
    r2 = lax.broadcasted_iota(jnp.int32, (2 * tq, tq), 0)
    c2 = lax.broadcasted_iota(jnp.int32, (2 * tq, tq), 1)
    suffix2 = (jnp.where(r2 >= tq, r2 - tq, r2) >= c2).astype(BF16)

    def head(hd):
        return slice(hd * HEAD_DIM, (hd + 1) * HEAD_DIM)

    def scores(qb, kb):
        z2 = lax.dot_general(qb, kb, (((1,), (1,)), ((), ())), preferred_element_type=F32)
        z2 = z2 * (scale * LOG2_E)
        sp = jnp.maximum(z2, 0.0) + jnp.log2(1.0 + jnp.exp2(-jnp.abs(z2)))
        return z2, sp

    def split(sp):
        hi = sp.astype(BF16)
        return hi, (sp - hi.astype(F32)).astype(BF16)

    heads = range(n_heads)

    def score_head(hd, slot, q0, k0, diagonal):
        z, sp = scores(q_ref[pl.ds(q0, tq), head(hd)], k_ref[pl.ds(k0, tq), head(hd)])
        hi, lo = split(jnp.where(strict, sp, 0.0) if diagonal else sp)
        z_buf[slot, hd] = z
        hl_buf[slot, hd, :, :tq] = hi
        hl_buf[slot, hd, :, tq:] = lo

    def weights_head(hd, slot, diagonal):
        s = jnp.dot(hl_buf[slot, hd], suffix2, preferred_element_type=F32)
        if diagonal:
            att = jnp.where(strict, jnp.exp2(z_buf[slot, hd] - s), 0.0)
            carry_buf[hd] = s[:, 0:1]
        else:
            att = jnp.exp2(z_buf[slot, hd] - s - carry_buf[hd])
            carry_buf[hd] += s[:, 0:1]
        return att.astype(BF16)

    def values_head(hd, att, k0, diagonal):
        pv = jnp.dot(att, v_ref[pl.ds(k0, tq), head(hd)], preferred_element_type=F32)
        if diagonal:
            acc_buf[hd] = pv
        else:
            acc_buf[hd] += pv

    def score_stage(slot, q0, k0, diagonal):
        for hd in heads:
            score_head(hd, slot, q0, k0, diagonal)

    def value_stage(slot, k0, diagonal):
        atts = [weights_head(hd, slot, diagonal) for hd in heads]
        for hd in heads:
            values_head(hd, atts[hd], k0, diagonal)

    def write_out(q0):
        for hd in heads:
            o_ref[pl.ds(q0, tq), head(hd)] = acc_buf[hd].astype(o_ref.dtype)

    score_stage(0, 0, 0, True)
    value_stage(0, 0, True)
    write_out(0)

    def q_block(qi, _):
        q0 = pl.multiple_of(qi * tq, tq)
        score_stage(1, q0, q0, True)
        k_first = pl.multiple_of((qi - 1) * tq, tq)
        for hd in heads:
            score_head(hd, 0, q0, k_first, False)
            values_head(hd, weights_head(hd, 1, True), q0, True)

        def kv_block(t, _):
            k_cur = pl.multiple_of((qi - t) * tq, tq)
            k_next = pl.multiple_of((qi - t - 1) * tq, tq)
            for cur in range(2):
                @pl.when((t - 1) % 2 == cur)
                def _step():
                    for hd in heads:
                        score_head(hd, 1 - cur, q0, k_next, False)
                        values_head(hd, weights_head(hd, cur, False), k_cur, False)
            return 0

        lax.fori_loop(1, qi, kv_block, 0)
        for cur in range(2):
            @pl.when((qi - 1) % 2 == cur)
            def _last():
                value_stage(cur, 0, False)
        write_out(q0)
        return 0

    lax.fori_loop(1, n_blk, q_block, 0)


def sb_attention(qkv, batch, seq):
    m, three_d = qkv.shape
    n_heads = three_d // (3 * HEAD_DIM)
    hb = min(SB_HEADS_PER_STEP, n_heads)
    n_groups = n_heads // hb
    width = hb * HEAD_DIM
    kern = functools.partial(_sb_attn_kernel, tq=TQ, scale=HEAD_DIM ** -0.5)
    return pl.pallas_call(
        kern,
        grid=(batch, n_groups),
        in_specs=[pl.BlockSpec((seq, width), lambda b, h: (b, h)),
                  pl.BlockSpec((seq, width), lambda b, h: (b, n_groups + h)),
                  pl.BlockSpec((seq, width), lambda b, h: (b, 2 * n_groups + h))],
        out_specs=pl.BlockSpec((seq, width), lambda b, h: (b, h)),
        out_shape=jax.ShapeDtypeStruct((m, n_heads * HEAD_DIM), BF16),
        scratch_shapes=[pltpu.VMEM((2, hb, TQ, TQ), F32), pltpu.VMEM((2, hb, TQ, 2 * TQ), BF16),
                        pltpu.VMEM((hb, TQ, HEAD_DIM), F32),
                        pltpu.VMEM((hb, TQ, 1), F32)],
        compiler_params=_params(2),
        name="sb_attention",
    )(qkv, qkv, qkv)


def stick_breaking_mixer(h, x, w_qkv, w_out, j, batch, seq):
    d = h.shape[1]
    qkv = matmul(h, w_qkv, w_lead=j, kb_x=0, kb_w=0, k=d, part_blocks=(0,), tn=TN_WIDE,
                 tm=TM_WIDE, n_out=w_qkv.shape[-1], out_dtype=BF16, epilogue=_ep_store, name="sb_qkv")
    o = sb_attention(qkv, batch, seq)
    return linear_residual(o, w_out, x, w_lead=j, k=w_out.shape[1], name="sb_out")


def swiglu_hidden(h, w_gu, w_lead, name):
    d = h.shape[1]
    f = w_gu.shape[-1] // 2
    tm, tn = 2 * TM, TN_PART
    return matmul(h, w_gu, w_lead=w_lead, kb_x=0, kb_w=0, k=d, part_blocks=(0, f // tn),
                  tn=tn, tm=tm, n_out=f, out_dtype=BF16, epilogue=_ep_swiglu, name=name)


def dense_swiglu(h, x, w_gu, w_down, f_idx):
    k_split = w_gu.shape[1]
    a = swiglu_hidden(h, w_gu, f_idx, "dense_gu")
    for kb in range(w_down.shape[1] // k_split):
        x = linear_residual(a, w_down, x, w_lead=f_idx, kb_x=kb, kb_w=kb, k=k_split,
                            name="dense_down")
    return x


def moe_routing(idx2, n_exp, tmg, n_col_tiles):
    m = idx2.shape[0]
    n_slots = 2 * m
    t_bound = n_slots // tmg + n_exp
    flat = idx2.reshape(-1)
    onehot = (flat[:, None] == jnp.arange(n_exp, dtype=jnp.int32)[None, :]).astype(jnp.int32)
    rank = jnp.sum((jnp.cumsum(onehot, axis=0) - onehot) * onehot, axis=1)
    counts = jnp.sum(onehot, axis=0)
    tiles = (counts + tmg - 1) // tmg
    tile_end = jnp.cumsum(tiles)
    tile_start = tile_end - tiles
    used = tile_end[-1]
    pos = tile_start[flat] * tmg + rank
    row_token = (jnp.arange(t_bound * tmg, dtype=jnp.int32) % m).at[pos].set(
        jnp.arange(n_slots, dtype=jnp.int32) // 2)

    def work_list(nt):
        item_end = jnp.cumsum(tiles * nt)
        total = item_end[-1]
        w = jnp.arange(nt * t_bound, dtype=jnp.int32)
        valid = w < total
        wc = jnp.minimum(w, total - 1)
        e = jnp.sum((item_end[None, :] <= wc[:, None]).astype(jnp.int32), axis=1)
        local = wc - (item_end - tiles * nt)[e]
        r = jnp.maximum(tiles[e], 1)
        n = local // r
        row = tile_start[e] + local % r
        first = valid & (local % r == 0)
        k = w - total
        out_row = jnp.where(valid, row, used + k // nt)
        out_col = jnp.where(valid, n, k % nt)
        return jnp.stack([e, n, row, out_row, out_col, valid.astype(jnp.int32),
                          first.astype(jnp.int32)]).astype(jnp.int32)

    return pos, row_token, [work_list(nt) for nt in n_col_tiles]


GATHER_UNROLL = 8
COMBINE_ROWS = 16


def _issue_row_gather(n_rows, src_index, src_hbm, dst_ref, sem):
    def body(group, _):
        for u in range(GATHER_UNROLL):
            r = GATHER_UNROLL * group + u
            pltpu.make_async_copy(src_hbm.at[pl.ds(src_index(r), 1)],
                                  dst_ref.at[pl.ds(r, 1)], sem).start(priority=u % 2)
        return 0
    lax.fori_loop(0, n_rows // GATHER_UNROLL, body, 0)


def _wait_row_gather(n_rows, src_hbm, dst_ref, sem):
    pltpu.make_async_copy(src_hbm.at[pl.ds(0, n_rows)], dst_ref, sem).wait()


def _gather_unpack_kernel(tok_ref, hp_hbm, o_ref, buf, sems, *, tg):
    i = pl.program_id(0)
    slot = i % 2

    def issue(step, s):
        _issue_row_gather(tg, lambda r: tok_ref[step * tg + r], hp_hbm, buf.at[s], sems.at[s])

    @pl.when(i == 0)
    def _prime():
        issue(0, 0)

    @pl.when(i + 1 < pl.num_programs(0))
    def _prefetch():
        issue(i + 1, 1 - slot)

    _wait_row_gather(tg, hp_hbm, buf.at[slot], sems.at[slot])
    packed = buf[slot]
    half = packed.shape[1]
    o_ref[:, :half] = pltpu.bitcast(packed << 16, F32).astype(o_ref.dtype)
    o_ref[:, half:] = pltpu.bitcast(packed & jnp.uint32(0xFFFF0000), F32).astype(o_ref.dtype)


def gather_rows(hp, row_token, tg=256):
    half = hp.shape[1]
    p = row_token.shape[0]
    grid_spec = pltpu.PrefetchScalarGridSpec(
        num_scalar_prefetch=1,
        grid=(p // tg,),
        in_specs=[pl.BlockSpec(memory_space=pl.ANY)],
        out_specs=pl.BlockSpec((tg, 2 * half), lambda i, tok: (i, 0)),
        scratch_shapes=[pltpu.VMEM((2, tg, half), jnp.uint32), pltpu.SemaphoreType.DMA((2,))])
    return pl.pallas_call(
        functools.partial(_gather_unpack_kernel, tg=tg),
        grid_spec=grid_spec,
        out_shape=jax.ShapeDtypeStruct((p, 2 * half), BF16),
        compiler_params=_params(1),
        name="moe_gather",
    )(row_token, hp)


def _grouped_mm_kernel(work_ref, x_ref, *refs, n_parts, epilogue):
    w_refs = refs[:n_parts]
    o_ref = refs[n_parts]
    wb_ref = refs[n_parts + 1]
    w = pl.program_id(0)

    @pl.when(work_ref[6, w] == 1)
    def _cast_weights():
        for p in range(n_parts):
            wb_ref[p] = w_refs[p][...].astype(BF16)

    @pl.when(work_ref[5, w] == 1)
    def _compute():
        x = x_ref[...]
        accs = [jnp.dot(x, wb_ref[p], preferred_element_type=F32) for p in range(n_parts)]
        epilogue(accs, (), o_ref, ())

    @pl.when(work_ref[5, w] == 0)
    def _zero_fill():
        o_ref[...] = jnp.zeros_like(o_ref)


def grouped_matmul(x, w, work, *, lead_base, k, part_blocks, tn, tmg, n_out, out_dtype,
                   epilogue, name):
    p = x.shape[0]
    n_parts = len(part_blocks)
    in_specs = [pl.BlockSpec((tmg, k), lambda i, wk: (wk[2, i], 0))]
    for pb in part_blocks:
        in_specs.append(pl.BlockSpec(
            (None, k, tn), lambda i, wk, pb=pb: (lead_base + wk[0, i], 0, pb + wk[1, i])))
    grid_spec = pltpu.PrefetchScalarGridSpec(
        num_scalar_prefetch=1,
        grid=(work.shape[1],),
        in_specs=in_specs,
        out_specs=pl.BlockSpec((tmg, tn), lambda i, wk: (wk[3, i], wk[4, i])),
        scratch_shapes=[pltpu.VMEM((n_parts, k, tn), BF16)])
    return pl.pallas_call(
        functools.partial(_grouped_mm_kernel, n_parts=n_parts, epilogue=epilogue),
        grid_spec=grid_spec,
        out_shape=jax.ShapeDtypeStruct((p, n_out), out_dtype),
        compiler_params=_params(1),
        name=name,
    )(work, x, *([w] * n_parts))


def _combine_kernel(pos_ref, x_ref, gate_ref, g_ref, y_hbm, *refs, tc, emit_x):
    if emit_x:
        o_ref, h_ref, buf, sems = refs
    else:
        h_ref, buf, sems = refs
    i = pl.program_id(0)
    slot = i % 2

    def issue(step, s):
        for k in range(2):
            _issue_row_gather(tc, lambda r, k=k: pos_ref[2 * (step * tc + r) + k], y_hbm,
                              buf.at[s, k], sems.at[s, k])

    @pl.when(i == 0)
    def _prime():
        issue(0, 0)

    @pl.when(i + 1 < pl.num_programs(0))
    def _prefetch():
        issue(i + 1, 1 - slot)

    for k in range(2):
        _wait_row_gather(tc, y_hbm, buf.at[slot, k], sems.at[slot, k])
    def chunk(c, _):
        rows = pl.ds(pl.multiple_of(c * COMBINE_ROWS, COMBINE_ROWS), COMBINE_ROWS)
        gates = gate_ref[rows, :]
        x = (x_ref[rows, :] + gates[:, 0:1] * buf[slot, 0, rows, :]
             + gates[:, 1:2] * buf[slot, 1, rows, :])
        if emit_x:
            o_ref[rows, :] = x
        ms = jnp.mean(x * x, axis=-1, keepdims=True)
        h_ref[rows, :] = (x * lax.rsqrt(ms + NORM_EPS) * g_ref[...]).astype(h_ref.dtype)
        return 0

    lax.fori_loop(0, tc // COMBINE_ROWS, chunk, 0, unroll=4)


def moe_combine(x, gates, y, pos, next_g, emit_x, tc=256):
    m, d = x.shape
    row_spec = pl.BlockSpec((tc, d), lambda i, pos: (i, 0))
    out_specs = [row_spec, row_spec] if emit_x else row_spec
    out_shape = ([jax.ShapeDtypeStruct((m, d), F32), jax.ShapeDtypeStruct((m, d), BF16)]
                 if emit_x else jax.ShapeDtypeStruct((m, d), F32))
    grid_spec = pltpu.PrefetchScalarGridSpec(
        num_scalar_prefetch=1,
        grid=(m // tc,),
        in_specs=[row_spec,
                  pl.BlockSpec((tc, LANES), lambda i, pos: (i, 0)),
                  pl.BlockSpec((1, d), lambda i, pos: (0, 0)),
                  pl.BlockSpec(memory_space=pl.ANY)],
        out_specs=out_specs,
        scratch_shapes=[pltpu.VMEM((2, 2, tc, d), F32), pltpu.SemaphoreType.DMA((2, 2))])
    return pl.pallas_call(
        functools.partial(_combine_kernel, tc=tc, emit_x=emit_x),
        grid_spec=grid_spec,
        out_shape=out_shape,
        compiler_params=_params(1),
        name="moe_combine",
    )(pos, x, gates, next_g.reshape(1, d), y)


def moe_swiglu(x, norm_g, w_router, w_gu, w_down, f_idx, next_g, emit_x):
    n_exp, d, two_f = w_gu.shape[1:]
    d_exp = two_f // 2
    tmg, tn_gu, tn_down = TMG, TN, min(d, 4 * TN)
    gates, idx, hp = router(x, norm_g, w_router)
    pos, row_token, (work_gu, work_down) = moe_routing(
        idx[:, :2], n_exp, tmg, (d_exp // tn_gu, d // tn_down))
    hs = gather_rows(hp, row_token)
    act = grouped_matmul(hs, w_gu.reshape(-1, d, two_f), work_gu, lead_base=f_idx * n_exp, k=d,
                         part_blocks=(0, d_exp // tn_gu), tn=tn_gu, tmg=tmg, n_out=d_exp,
                         out_dtype=BF16, epilogue=_ep_swiglu, name="moe_gu")
    y = grouped_matmul(act, w_down.reshape(-1, d_exp, d), work_down, lead_base=f_idx * n_exp,
                       k=d_exp, part_blocks=(0,), tn=tn_down, tmg=tmg, n_out=d, out_dtype=F32,
                       epilogue=_ep_store, name="moe_down")
    return moe_combine(x, gates, y, pos, next_g, emit_x)


def kernel(x, mix_norm, ffn_norm, conv_w_in, conv_w, conv_w_out, gmlp_w_in, gmlp_v_norm_g,
           gmlp_v_norm_b, gmlp_w_s, gmlp_b_s, gmlp_w_out, sb_w_qkv, sb_w_out, dense_w_gu,
           dense_w_down, moe_w_router, moe_w_gu, moe_w_down, final_norm):
    batch, seq, d = x.shape
    depth = mix_norm.shape[0]
    x = x.reshape(batch * seq, d)
    h = None
    for i in range(depth):
        if h is None:
            h = rmsnorm(x, mix_norm[i], BF16)
        kind, j = i % 3, i // 3
        if kind == 0:
            x = short_conv_mixer(h, x, conv_w_in, conv_w, conv_w_out, j, seq)
        elif kind == 1:
            x = chunked_gmlp_mixer(h, x, gmlp_w_in, gmlp_v_norm_g, gmlp_v_norm_b, gmlp_w_s,
                                   gmlp_b_s, gmlp_w_out, j)
        else:
            x = stick_breaking_mixer(h, x, sb_w_qkv, sb_w_out, j, batch, seq)
        f = i // 2
        last = i == depth - 1
        if i % 2 == 0:
            h = rmsnorm(x, ffn_norm[i], BF16)
            x = dense_swiglu(h, x, dense_w_gu, dense_w_down, f)
            h = None
        elif last:
            out = moe_swiglu(x, ffn_norm[i], moe_w_router[f], moe_w_gu, moe_w_down, f,
                             final_norm, emit_x=False)
            return out.reshape(batch, seq, d)
        else:
            x, h = moe_swiglu(x, ffn_norm[i], moe_w_router[f], moe_w_gu, moe_w_down, f,
                              mix_norm[i + 1], emit_x=True)
    return rmsnorm(x, final_norm, F32).reshape(batch, seq, d)
```
